```python
import math
import jax, jax.numpy as jnp
from jax import lax
import numpy as np

D_MODEL = 2048
BATCH = 2
SEQ = 4096
DEPTH = 4

N_MIXERS = 2
N_HEADS = 16
HEAD_DIM = D_MODEL // N_HEADS
D_FF = 5632
CONV_WIDTH = 31
MOBA_BLOCK = 256
MOBA_TOPK = 3
Q_CHUNK = 64
NUM_BUCKETS = 32
MAX_DISTANCE = 2048
NORM_EPS = 1e-6
NEG_INF = -1e30
N_CONV_LAYERS = (DEPTH + 1) // 2
N_ATTN_LAYERS = DEPTH // 2

kernel_name = "hybrid_conformer_conv_moba_macaron"


def rms_norm(x, g):
    xf = x.astype(jnp.float32)
    y = xf * lax.rsqrt(jnp.mean(xf * xf, axis=-1, keepdims=True) + NORM_EPS)
    return (y * g.astype(jnp.float32)).astype(x.dtype)


def layer_norm(x, g, b):
    xf = x.astype(jnp.float32)
    mu = jnp.mean(xf, axis=-1, keepdims=True)
    xc = xf - mu
    y = xc * lax.rsqrt(jnp.mean(xc * xc, axis=-1, keepdims=True) + NORM_EPS)
    return (y * g.astype(jnp.float32) + b.astype(jnp.float32)).astype(x.dtype)


def swiglu_ffn(h, w_gate, w_up, w_down):
    return (jax.nn.silu(h @ w_gate) * (h @ w_up)) @ w_down


def conformer_conv(h, pw1_w, pw1_b, dw_w, dw_b, ln_g, ln_b, pw2_w, pw2_b):
    u = h @ pw1_w + pw1_b
    a, g = jnp.split(u, 2, axis=-1)
    u = a * jax.nn.sigmoid(g)
    u = lax.conv_general_dilated(
        u, dw_w[:, None, :].astype(u.dtype), window_strides=(1,),
        padding=[(CONV_WIDTH - 1, 0)],
        dimension_numbers=("NWC", "WIO", "NWC"),
        feature_group_count=D_MODEL) + dw_b
    u = jax.nn.silu(layer_norm(u, ln_g, ln_b))
    return u @ pw2_w + pw2_b


def t5_bucket(dist):
    max_exact = NUM_BUCKETS // 2
    n = jnp.maximum(dist, 0)
    nf = jnp.maximum(n, 1).astype(jnp.float32)
    large = max_exact + (jnp.log(nf / max_exact) / math.log(MAX_DISTANCE / max_exact)
                         * (NUM_BUCKETS - max_exact)).astype(jnp.int32)
    large = jnp.minimum(large, NUM_BUCKETS - 1)
    return jnp.where(n < max_exact, n, large)


_gather_blocks = jax.vmap(jax.vmap(lambda blocks, idx: blocks[idx]))


def moba_attention(h, wqkv, q_norm, k_norm, wo, rel_bias):
    B, S, _ = h.shape
    nb = -(-S // MOBA_BLOCK)
    s_pad = nb * MOBA_BLOCK
    f32 = jnp.float32
    qkv = (h @ wqkv).reshape(B, S, 3, N_HEADS, HEAD_DIM)
    q = rms_norm(qkv[:, :, 0], q_norm)
    k = rms_norm(qkv[:, :, 1], k_norm)
    v = qkv[:, :, 2]

    def to_bhsd(t):
        return jnp.pad(t.transpose(0, 2, 1, 3), ((0, 0), (0, 0), (0, s_pad - S), (0, 0)))

    q, k, v = to_bhsd(q), to_bhsd(k), to_bhsd(v)
    kb = k.reshape(B, N_HEADS, nb, MOBA_BLOCK, HEAD_DIM)
    vb = v.reshape(B, N_HEADS, nb, MOBA_BLOCK, HEAD_DIM)

    k_mean = jnp.mean(kb.astype(f32), axis=3)
    gate = jnp.einsum("bhsd,bhnd->bhsn", q.astype(f32), k_mean)
    pos = jnp.arange(s_pad)
    own_blk = pos // MOBA_BLOCK
    past = jnp.arange(nb)[None, :] < own_blk[:, None]
    gate = jnp.where(past, gate, -jnp.inf)
    k_sel = min(MOBA_TOPK, nb)
    _, sel = lax.top_k(gate, k_sel)
    sel_valid = jnp.arange(k_sel)[None, :] < own_blk[:, None]

    scale = HEAD_DIM ** -0.5
    bias_table = rel_bias.T.astype(f32)
    head_idx = jnp.arange(N_HEADS)[None, :, None, None]
    offs = jnp.arange(MOBA_BLOCK)

    def chunk(c):
        q0 = c * Q_CHUNK
        qc = lax.dynamic_slice_in_dim(q, q0, Q_CHUNK, axis=2)
        q_pos = q0 + jnp.arange(Q_CHUNK)
        blk = q0 // MOBA_BLOCK
        sel_c = lax.dynamic_slice_in_dim(sel, q0, Q_CHUNK, axis=2)
        valid_c = lax.dynamic_slice_in_dim(sel_valid, q0, Q_CHUNK, axis=0)
        logits = []
        for r in range(k_sel):
            idx = sel_c[..., r]
            k_r = _gather_blocks(kb, idx)
            s_r = jnp.einsum("bhqd,bhqkd->bhqk", qc, k_r, preferred_element_type=f32)
            k_pos = idx[..., None] * MOBA_BLOCK + offs
            bias = bias_table[head_idx, t5_bucket(q_pos[None, None, :, None] - k_pos)]
            logits.append(jnp.where(valid_c[None, None, :, r, None], s_r * scale + bias, NEG_INF))
        k_own = lax.dynamic_slice_in_dim(kb, blk, 1, axis=2)[:, :, 0]
        v_own = lax.dynamic_slice_in_dim(vb, blk, 1, axis=2)[:, :, 0]
        s_o = jnp.einsum("bhqd,bhkd->bhqk", qc, k_own, preferred_element_type=f32)
        dist = q_pos[:, None] - (blk * MOBA_BLOCK + offs)[None, :]
        bias_o = bias_table[:, t5_bucket(dist)]
        logits.append(jnp.where(dist >= 0, s_o * scale + bias_o[None], NEG_INF))
        probs = jax.nn.softmax(jnp.concatenate(logits, axis=-1), axis=-1)
        p_parts = jnp.split(probs.astype(v.dtype), k_sel + 1, axis=-1)
        out = jnp.einsum("bhqk,bhkd->bhqd", p_parts[-1], v_own, preferred_element_type=f32)
        for r in range(k_sel):
            v_r = _gather_blocks(vb, sel_c[..., r])
            out = out + jnp.einsum("bhqk,bhqkd->bhqd", p_parts[r], v_r, preferred_element_type=f32)
        return out.astype(h.dtype)

    n_chunks = s_pad // Q_CHUNK
    out = lax.map(chunk, jnp.arange(n_chunks))
    out = out.transpose(1, 0, 3, 2, 4).reshape(B, s_pad, D_MODEL)[:, :S]
    return out @ wo


def _dense(key, shape, fan_in):
    return jax.random.normal(key, shape, jnp.float32) * (fan_in ** -0.5)


def _gain(key, shape):
    return 1.0 + 0.01 * jax.random.normal(key, shape, jnp.float32)


def _small(key, shape, s=0.01):
    return s * jax.random.normal(key, shape, jnp.float32)


def setup_inputs(seed: int = 0) -> dict:
    key = jax.random.key(seed)
    ks = jax.random.split(key, 24)
    L, NC, NA, D, F = DEPTH, N_CONV_LAYERS, N_ATTN_LAYERS, D_MODEL, D_FF
    return {
        "x": jax.random.normal(ks[0], (BATCH, SEQ, D), jnp.float32),
        "rel_bias": _small(ks[1], (NUM_BUCKETS, N_HEADS), 0.1),
        "ffn1_norm": _gain(ks[2], (L, D)),
        "ffn1_w_gate": _dense(ks[3], (L, D, F), D),
        "ffn1_w_up": _dense(ks[4], (L, D, F), D),
        "ffn1_w_down": _dense(ks[5], (L, F, D), F),
        "mix_norm": _gain(ks[6], (L, D)),
        "ffn2_norm": _gain(ks[7], (L, D)),
        "ffn2_w_gate": _dense(ks[8], (L, D, F), D),
        "ffn2_w_up": _dense(ks[9], (L, D, F), D),
        "ffn2_w_down": _dense(ks[10], (L, F, D), F),
        "conv_pw1_w": _dense(ks[11], (NC, D, 2 * D), D),
        "conv_pw1_b": _small(ks[12], (NC, 2 * D)),
        "conv_dw_w": _dense(ks[13], (NC, CONV_WIDTH, D), CONV_WIDTH),
        "conv_dw_b": _small(ks[14], (NC, D)),
        "conv_ln_g": _gain(ks[15], (NC, D)),
        "conv_ln_b": _small(ks[16], (NC, D)),
        "conv_pw2_w": _dense(ks[17], (NC, D, D), D),
        "conv_pw2_b": _small(ks[18], (NC, D)),
        "attn_wqkv": _dense(ks[19], (NA, D, 3 * D), D),
        "attn_q_norm": _gain(ks[20], (NA, HEAD_DIM)),
        "attn_k_norm": _gain(ks[21], (NA, HEAD_DIM)),
        "attn_wo": _dense(ks[22], (NA, D, D), D),
    }


def reference(x, rel_bias, ffn1_norm, ffn1_w_gate, ffn1_w_up, ffn1_w_down, mix_norm,
              ffn2_norm, ffn2_w_gate, ffn2_w_up, ffn2_w_down,
              conv_pw1_w, conv_pw1_b, conv_dw_w, conv_dw_b, conv_ln_g, conv_ln_b,
              conv_pw2_w, conv_pw2_b, attn_wqkv, attn_q_norm, attn_k_norm, attn_wo):
    for i in range(DEPTH):
        x = x + 0.5 * swiglu_ffn(rms_norm(x, ffn1_norm[i]), ffn1_w_gate[i], ffn1_w_up[i], ffn1_w_down[i])
        h = rms_norm(x, mix_norm[i])
        j = i // N_MIXERS
        if i % N_MIXERS == 0:
            x = x + conformer_conv(h, conv_pw1_w[j], conv_pw1_b[j], conv_dw_w[j], conv_dw_b[j],
                                   conv_ln_g[j], conv_ln_b[j], conv_pw2_w[j], conv_pw2_b[j])
        else:
            x = x + moba_attention(h, attn_wqkv[j], attn_q_norm[j], attn_k_norm[j], attn_wo[j], rel_bias)
        x = x + 0.5 * swiglu_ffn(rms_norm(x, ffn2_norm[i]), ffn2_w_gate[i], ffn2_w_up[i], ffn2_w_down[i])
    return x
```

```python
import functools
import math

import jax
import jax.numpy as jnp
from jax import lax
from jax.experimental import pallas as pl
from jax.experimental.pallas import tpu as pltpu

D_MODEL = 2048
N_HEADS = 16
HEAD_DIM = D_MODEL // N_HEADS
D_FF = 5632
CONV_WIDTH = 31
MOBA_BLOCK = 256
MOBA_TOPK = 3
NUM_BUCKETS = 32
MAX_DISTANCE = 2048
NORM_EPS = 1e-6
NEG_INF = -1e30
N_MIXERS = 2

LANES = 128
SUBLANES = 8
VMEM_LIMIT_BYTES = 60000 * 1024

BF16 = jnp.bfloat16
F32 = jnp.float32

FFN_TM = 512
FFN_TF = 512
PROJ_TM = 1024
PROJ_TN = 512
OUT_TM = 512
CONV_TS = 256
CONV_HALO = 32
CONV_ROWS = 64


def _params(*semantics):
    return pltpu.CompilerParams(
        dimension_semantics=semantics, vmem_limit_bytes=VMEM_LIMIT_BYTES)


def _rms_norm_rows(x, g):
    ms = jnp.mean(x * x, axis=-1, keepdims=True)
    return (x * lax.rsqrt(ms + NORM_EPS)) * g


def _sigmoid(x):
    return 1.0 / (1.0 + jnp.exp(-x))


def _ffn_kernel(x_ref, g_ref, wg_ref, wu_ref, wd_ref, o_ref, h_ref):
    f = pl.program_id(1)

    @pl.when(f == 0)
    def _():
        x = x_ref[...]
        h_ref[...] = _rms_norm_rows(x, g_ref[...]).astype(BF16)
        o_ref[...] = x

    h = h_ref[...]
    gate = jnp.dot(h, wg_ref[...], preferred_element_type=F32)
    up = jnp.dot(h, wu_ref[...], preferred_element_type=F32)
    act = (gate * _sigmoid(gate)) * up * 0.5
    o_ref[...] += jnp.dot(act.astype(BF16), wd_ref[...], preferred_element_type=F32)


def _ffn(x, norm_g, w_gate, w_up, w_down, layer):
    t, d = x.shape
    f = w_gate.shape[-1]
    tm, tf = FFN_TM, FFN_TF
    return pl.pallas_call(
        _ffn_kernel,
        grid=(t // tm, f // tf),
        in_specs=[
            pl.BlockSpec((tm, d), lambda i, j: (i, 0)),
            pl.BlockSpec((None, 1, d), lambda i, j: (layer, 0, 0)),
            pl.BlockSpec((None, d, tf), lambda i, j: (layer, 0, j)),
            pl.BlockSpec((None, d, tf), lambda i, j: (layer, 0, j)),
            pl.BlockSpec((None, tf, d), lambda i, j: (layer, j, 0)),
        ],
        out_specs=pl.BlockSpec((tm, d), lambda i, j: (i, 0)),
        out_shape=jax.ShapeDtypeStruct((t, d), F32),
        scratch_shapes=[pltpu.VMEM((tm, d), BF16)],
        compiler_params=_params("parallel", "arbitrary"),
        name="ffn",
    )(x, norm_g, w_gate, w_up, w_down)


def _pw1_kernel(x_ref, g_ref, wa_ref, wg_ref, ba_ref, bg_ref, o_ref, h_ref):
    @pl.when(pl.program_id(1) == 0)
    def _():
        h_ref[...] = _rms_norm_rows(x_ref[...], g_ref[...]).astype(BF16)

    h = h_ref[...]
    a = jnp.dot(h, wa_ref[...], preferred_element_type=F32) + ba_ref[...]
    g = jnp.dot(h, wg_ref[...], preferred_element_type=F32) + bg_ref[...]
    o_ref[...] = a * _sigmoid(g)


def _pw1_glu(x, norm_g, w, b, layer, conv_layer):
    t, d = x.shape
    tm, tn = PROJ_TM, PROJ_TN
    nj = d // tn
    return pl.pallas_call(
        _pw1_kernel,
        grid=(t // tm, nj),
        in_specs=[
            pl.BlockSpec((tm, d), lambda i, j: (i, 0)),
            pl.BlockSpec((None, 1, d), lambda i, j: (layer, 0, 0)),
            pl.BlockSpec((None, d, tn), lambda i, j: (conv_layer, 0, j)),
            pl.BlockSpec((None, d, tn), lambda i, j: (conv_layer, 0, j + nj)),
            pl.BlockSpec((None, 1, tn), lambda i, j: (conv_layer, 0, j)),
            pl.BlockSpec((None, 1, tn), lambda i, j: (conv_layer, 0, j + nj)),
        ],
        out_specs=pl.BlockSpec((tm, tn), lambda i, j: (i, j)),
        out_shape=jax.ShapeDtypeStruct((t, d), F32),
        scratch_shapes=[pltpu.VMEM((tm, d), BF16)],
        compiler_params=_params("parallel", "arbitrary"),
        name="conv_pw1_glu",
    )(x, norm_g, w, w, b, b)


def _conv_kernel(u_ref, halo_ref, dww_ref, dwb_ref, lng_ref, lnb_ref, w2_ref,
                 b2_ref, x_ref, o_ref, ext_ref, cv_ref, y_ref):
    ts, d = u_ref.shape
    n_slabs = d // LANES
    pad = CONV_HALO - (CONV_WIDTH - 1)
    first = pl.program_id(1) == 0

    for c in range(n_slabs):
        lanes = slice(c * LANES, (c + 1) * LANES)
        halo = halo_ref[:, lanes]
        ext_ref[c, 0:CONV_HALO, :] = jnp.where(first, 0.0, halo)
        ext_ref[c, CONV_HALO:, :] = u_ref[:, lanes]

    def slab_body(c, carry):
        w_rows = [dww_ref[c, k:k + 1, :] for k in range(CONV_WIDTH)]
        bias = dwb_ref[c]

        def row_body(r, carry2):
            r0 = pl.multiple_of(r * CONV_ROWS, CONV_ROWS)
            acc = jnp.broadcast_to(bias, (CONV_ROWS, LANES))
            for k in range(CONV_WIDTH):
                acc = acc + w_rows[k] * ext_ref[c, pl.ds(r0 + pad + k, CONV_ROWS), :]
            cv_ref[c, pl.ds(r0, CONV_ROWS), :] = acc
            return carry2

        return lax.fori_loop(0, ts // CONV_ROWS, row_body, carry)

    lax.fori_loop(0, n_slabs, slab_body, 0)

    total = cv_ref[0]
    for c in range(1, n_slabs):
        total = total + cv_ref[c]
    mu = jnp.sum(total, axis=-1, keepdims=True) * (1.0 / d)
    sq = jnp.zeros((ts, LANES), F32)
    for c in range(n_slabs):
        xc = cv_ref[c] - mu
        sq = sq + xc * xc
    rstd = lax.rsqrt(jnp.sum(sq, axis=-1, keepdims=True) * (1.0 / d) + NORM_EPS)
    for c in range(n_slabs):
        lanes = slice(c * LANES, (c + 1) * LANES)
        y = (cv_ref[c] - mu) * rstd * lng_ref[:, lanes] + lnb_ref[:, lanes]
        y_ref[:, lanes] = (y * _sigmoid(y)).astype(BF16)

    o_ref[...] = (x_ref[...] + b2_ref[...]
                  + jnp.dot(y_ref[...], w2_ref[...], preferred_element_type=F32))


def _conv_module(u, x, dw_w, dw_b, ln_g, ln_b, w2, b2, conv_layer, batch):
    t, d = x.shape
    s = t // batch
    ts = CONV_TS
    n_slabs = d // LANES
    spb = s // ts
    hpt = ts // CONV_HALO

    def halo_map(b, i):
        return (jnp.maximum((b * spb + i) * hpt - 1, 0), 0)

    row = lambda b, i: (b * spb + i, 0)
    vec = lambda b, i: (conv_layer, 0, 0)
    return pl.pallas_call(
        _conv_kernel,
        grid=(batch, spb),
        in_specs=[
            pl.BlockSpec((ts, d), row),
            pl.BlockSpec((CONV_HALO, d), halo_map),
            pl.BlockSpec((None, n_slabs, CONV_WIDTH, LANES), lambda b, i: (conv_layer, 0, 0, 0)),
            pl.BlockSpec((None, n_slabs, 1, LANES), lambda b, i: (conv_layer, 0, 0, 0)),
            pl.BlockSpec((None, 1, d), vec),
            pl.BlockSpec((None, 1, d), vec),
            pl.BlockSpec((None, d, d), vec),
            pl.BlockSpec((None, 1, d), vec),
            pl.BlockSpec((ts, d), row),
        ],
        out_specs=pl.BlockSpec((ts, d), row),
        out_shape=jax.ShapeDtypeStruct((t, d), F32),
        scratch_shapes=[
            pltpu.VMEM((n_slabs, ts + CONV_HALO, LANES), F32),
            pltpu.VMEM((n_slabs, ts, LANES), F32),
            pltpu.VMEM((ts, d), BF16),
        ],
        compiler_params=_params("parallel", "arbitrary"),
        name="conv_dw_ln_pw2",
    )(u, u, dw_w, dw_b, ln_g, ln_b, w2, b2, x)


def _qkv_kernel(x_ref, g_ref, w_ref, qn_ref, kn_ref, o_ref, h_ref):
    j = pl.program_id(1)
    heads_per_step = o_ref.shape[0]
    n_q_steps = N_HEADS // heads_per_step

    @pl.when(j == 0)
    def _():
        h_ref[...] = _rms_norm_rows(x_ref[...], g_ref[...]).astype(BF16)

    y = jnp.dot(h_ref[...], w_ref[...], preferred_element_type=F32)
    is_q = j < n_q_steps
    is_v = j >= 2 * n_q_steps
    gain = jnp.where(is_q, qn_ref[...] * (HEAD_DIM ** -0.5), kn_ref[...])
    for c in range(heads_per_step):
        t = y[:, c * HEAD_DIM:(c + 1) * HEAD_DIM]
        normed = _rms_norm_rows(t, gain)
        o_ref[c] = jnp.where(is_v, t, normed).astype(BF16)


def _qkv_proj(x, norm_g, w, q_norm, k_norm, layer, attn_layer):
    t, d = x.shape
    tm, tn = PROJ_TM, PROJ_TN
    hps = tn // HEAD_DIM
    return pl.pallas_call(
        _qkv_kernel,
        grid=(t // tm, 3 * d // tn),
        in_specs=[
            pl.BlockSpec((tm, d), lambda i, j: (i, 0)),
            pl.BlockSpec((None, 1, d), lambda i, j: (layer, 0, 0)),
            pl.BlockSpec((None, d, tn), lambda i, j: (attn_layer, 0, j)),
            pl.BlockSpec((None, 1, HEAD_DIM), lambda i, j: (attn_layer, 0, 0)),
            pl.BlockSpec((None, 1, HEAD_DIM), lambda i, j: (attn_layer, 0, 0)),
        ],
        out_specs=pl.BlockSpec((hps, tm, HEAD_DIM), lambda i, j: (j, i, 0)),
        out_shape=jax.ShapeDtypeStruct((3 * N_HEADS, t, HEAD_DIM), BF16),
        scratch_shapes=[pltpu.VMEM((tm, d), BF16)],
        compiler_params=_params("parallel", "arbitrary"),
        name="attn_qkv",
    )(x, norm_g, w, q_norm, k_norm)


def _t5_bucket(dist):
    max_exact = NUM_BUCKETS // 2
    n = jnp.maximum(dist, 0)
    nf = jnp.maximum(n, 1).astype(F32)
    large = max_exact + (jnp.log(nf / max_exact) / math.log(MAX_DISTANCE / max_exact)
                         * (NUM_BUCKETS - max_exact)).astype(jnp.int32)
    large = jnp.minimum(large, NUM_BUCKETS - 1)
    return jnp.where(n < max_exact, n, large)


def _attn_kernel(tbl_ref, bucket_ref, q_ref, k_ref, v_ref, o_ref,
                 bias_ref, kx_ref, kmean_ref):
    h = pl.program_id(1)
    i = pl.program_id(2)
    blk = MOBA_BLOCK
    s_len = k_ref.shape[0]
    nb = s_len // blk

    @pl.when(i == 0)
    def _():
        for dl in range(nb):
            bk = bucket_ref[dl:dl + 1, :]
            w = jnp.zeros(bk.shape, F32)
            for b in range(NUM_BUCKETS):
                w = jnp.where(bk == b, tbl_ref[h, b], w)
            rows = jnp.broadcast_to(w, (blk, 2 * blk))
            rolled = pltpu.roll(rows, blk + 1, 1, stride=1, stride_axis=0)
            bias_ref[dl] = rolled[:, :blk]
        k = k_ref[...]
        row_blk = lax.shift_right_logical(
            lax.broadcasted_iota(jnp.int32, (s_len, HEAD_DIM), 0), int(math.log2(blk)))
        lane = lax.broadcasted_iota(jnp.int32, (s_len, HEAD_DIM), 1)
        kx_ref[:, :HEAD_DIM] = k
        kx_ref[:, HEAD_DIM:] = jnp.where(row_blk == lane, 1.0, 0.0).astype(BF16)
        for n in range(nb):
            kb = k_ref[n * blk:(n + 1) * blk, :].astype(F32)
            kmean_ref[n:n + 1, :] = jnp.sum(kb, axis=0, keepdims=True) * (1.0 / blk)

    q = q_ref[...]
    gate = lax.dot_general(kmean_ref[...], q.astype(F32), (((1,), (1,)), ((), ())),
                           precision=lax.Precision.HIGHEST,
                           preferred_element_type=F32)
    n_idx = lax.broadcasted_iota(jnp.int32, gate.shape, 0)
    cnt = jnp.zeros(gate.shape, jnp.int32)
    for n in range(nb):
        row = gate[n:n + 1, :]
        beats = jnp.where(row > gate, 1, jnp.where((row == gate) & (n < n_idx), 1, 0))
        cnt = cnt + jnp.where(n < i, beats, 0)
    keep = ((n_idx < i) & (cnt < MOBA_TOPK)) | (n_idx == i)
    pen = jnp.where(keep, 0.0, NEG_INF)
    pen = jnp.concatenate([pen, jnp.zeros((HEAD_DIM - nb, blk), F32)], axis=0)
    qx = jnp.concatenate([q, pen.T.astype(BF16)], axis=1)

    def scores(n):
        start = pl.multiple_of(n * blk, blk)
        kx = kx_ref[pl.ds(start, blk), :]
        s = lax.dot_general(qx, kx, (((1,), (1,)), ((), ())), preferred_element_type=F32)
        return s, v_ref[pl.ds(start, blk), :]

    s, v = scores(i)
    s = s + bias_ref[0]
    r_idx = lax.broadcasted_iota(jnp.int32, s.shape, 0)
    c_idx = lax.broadcasted_iota(jnp.int32, s.shape, 1)
    s = jnp.where(r_idx >= c_idx, s, NEG_INF)
    m = jnp.max(s, axis=-1, keepdims=True)
    p = jnp.exp(s - m)
    l = jnp.sum(p, axis=-1, keepdims=True)
    acc = jnp.dot(p.astype(BF16), v, preferred_element_type=F32)

    def body(n, carry):
        m, l, acc = carry
        s, v = scores(n)
        s = s + bias_ref[i - n]
        m_new = jnp.maximum(m, jnp.max(s, axis=-1, keepdims=True))
        alpha = jnp.exp(m - m_new)
        p = jnp.exp(s - m_new)
        l = alpha * l + jnp.sum(p, axis=-1, keepdims=True)
        acc = alpha * acc + jnp.dot(p.astype(BF16), v, preferred_element_type=F32)
        return m_new, l, acc

    m, l, acc = lax.fori_loop(0, i, body, (m, l, acc))
    o_ref[...] = (acc / l).astype(BF16)


def _moba_attention(qkv, rel_bias, batch):
    _, t, hd = qkv.shape
    s = t // batch
    blk = MOBA_BLOCK
    nb = s // blk
    dl = jnp.arange(nb, dtype=jnp.int32)[:, None]
    n = jnp.arange(2 * blk, dtype=jnp.int32)[None, :]
    buckets = _t5_bucket(dl * blk + (blk - 1) - n)
    return pl.pallas_call(
        _attn_kernel,
        grid=(batch, N_HEADS, nb),
        in_specs=[
            pl.BlockSpec(memory_space=pltpu.SMEM),
            pl.BlockSpec((nb, 2 * blk), lambda b, h, i: (0, 0)),
            pl.BlockSpec((None, blk, hd), lambda b, h, i: (h, b * nb + i, 0)),
            pl.BlockSpec((None, s, hd), lambda b, h, i: (N_HEADS + h, b, 0)),
            pl.BlockSpec((None, s, hd), lambda b, h, i: (2 * N_HEADS + h, b, 0)),
        ],
        out_specs=pl.BlockSpec((blk, hd), lambda b, h, i: (b * nb + i, h)),
        out_shape=jax.ShapeDtypeStruct((t, N_HEADS * hd), BF16),
        scratch_shapes=[
            pltpu.VMEM((nb, blk, blk), F32),
            pltpu.VMEM((s, 2 * hd), BF16),
            pltpu.VMEM((nb, hd), F32),
        ],
        compiler_params=_params("parallel", "parallel", "arbitrary"),
        name="moba_attention",
    )(rel_bias.T.astype(F32), buckets, qkv, qkv, qkv)


def _wo_kernel(a_ref, w_ref, x_ref, o_ref):
    o_ref[...] = x_ref[...] + jnp.dot(a_ref[...], w_ref[...], preferred_element_type=F32)


def _wo_proj(a, x, w, attn_layer):
    t, d = x.shape
    tm = OUT_TM
    return pl.pallas_call(
        _wo_kernel,
        grid=(t // tm,),
        in_specs=[
            pl.BlockSpec((tm, d), lambda i: (i, 0)),
            pl.BlockSpec((None, d, d), lambda i: (attn_layer, 0, 0)),
            pl.BlockSpec((tm, d), lambda i: (i, 0)),
        ],
        out_specs=pl.BlockSpec((tm, d), lambda i: (i, 0)),
        out_shape=jax.ShapeDtypeStruct((t, d), F32),
        compiler_params=_params("parallel"),
        name="attn_wo",
    )(a, w, x)


def kernel(x, rel_bias, ffn1_norm, ffn1_w_gate, ffn1_w_up, ffn1_w_down, mix_norm, ffn2_norm, ffn2_w_gate, ffn2_w_up, ffn2_w_down, conv_pw1_w, conv_pw1_b, conv_dw_w, conv_dw_b, conv_ln_g, conv_ln_b, conv_pw2_w, conv_pw2_b, attn_wqkv, attn_q_norm, attn_k_norm, attn_wo):
    batch, seq, d = x.shape
    depth = ffn1_norm.shape[0]
    n_slabs = d // LANES
    bf = lambda w: w.astype(BF16)
    vec = lambda v: v.reshape(v.shape[0], 1, v.shape[-1])
    ffn1 = (vec(ffn1_norm), bf(ffn1_w_gate), bf(ffn1_w_up), bf(ffn1_w_down))
    ffn2 = (vec(ffn2_norm), bf(ffn2_w_gate), bf(ffn2_w_up), bf(ffn2_w_down))
    mix_g = vec(mix_norm)
    pw1_w, pw1_b = bf(conv_pw1_w), vec(conv_pw1_b)
    nc = conv_dw_w.shape[0]
    dw_w = conv_dw_w.reshape(nc, CONV_WIDTH, n_slabs, LANES).transpose(0, 2, 1, 3)
    dw_b = conv_dw_b.reshape(nc, n_slabs, 1, LANES)
    pw2_w = bf(conv_pw2_w)
    wqkv, wo = bf(attn_wqkv), bf(attn_wo)

    xt = x.reshape(batch * seq, d)
    for i in range(depth):
        xt = _ffn(xt, *ffn1, i)
        j = i // N_MIXERS
        if i % N_MIXERS == 0:
            u = _pw1_glu(xt, mix_g, pw1_w, pw1_b, i, j)
            xt = _conv_module(u, xt, dw_w, dw_b, vec(conv_ln_g), vec(conv_ln_b),
                              pw2_w, vec(conv_pw2_b), j, batch)
        else:
            qkv = _qkv_proj(xt, mix_g, wqkv, vec(attn_q_norm), vec(attn_k_norm), i, j)
            a = _moba_attention(qkv, rel_bias, batch)
            xt = _wo_proj(a, xt, wo, j)
        xt = _ffn(xt, *ffn2, i)
    return xt.reshape(batch, seq, d)
```

```python
import functools
import math

import jax
import jax.numpy as jnp
from jax import lax
from jax.experimental import pallas as pl
from jax.experimental.pallas import tpu as pltpu

D_MODEL = 2048
N_HEADS = 16
HEAD_DIM = D_MODEL // N_HEADS
D_FF = 5632
CONV_WIDTH = 31
MOBA_BLOCK = 256
MOBA_TOPK = 3
NUM_BUCKETS = 32
MAX_DISTANCE = 2048
NORM_EPS = 1e-6
NEG_INF = -1e30
N_MIXERS = 2

LANES = 128
SUBLANES = 8
VMEM_LIMIT_BYTES = 60000 * 1024

BF16 = jnp.bfloat16
F32 = jnp.float32

FFN_TM = 512
FFN_TF = 512
PROJ_TM = 1024
PROJ_TN = 512
OUT_TM = 512
CONV_TS = 256
CONV_HALO = 32
CONV_ROWS = 64
ATTN_HEADS_PER_STEP = 4
ATTN_CHUNK_BLOCKS = 2
LOG2E = math.log2(math.e)


def _params(*semantics):
    return pltpu.CompilerParams(
        dimension_semantics=semantics, vmem_limit_bytes=VMEM_LIMIT_BYTES)


def _rms_norm_rows(x, g):
    ms = jnp.mean(x * x, axis=-1, keepdims=True)
    return (x * lax.rsqrt(ms + NORM_EPS)) * g


def _sigmoid(x):
    return 1.0 / (1.0 + jnp.exp(-x))


def _ffn_kernel(x_ref, g_ref, wg_ref, wu_ref, wd_ref, o_ref, h_ref):
    f = pl.program_id(1)

    @pl.when(f == 0)
    def _():
        x = x_ref[...]
        h_ref[...] = _rms_norm_rows(x, g_ref[...]).astype(BF16)
        o_ref[...] = x

    h = h_ref[...]
    gate = jnp.dot(h, wg_ref[...], preferred_element_type=F32)
    up = jnp.dot(h, wu_ref[...], preferred_element_type=F32)
    act = (gate * _sigmoid(gate)) * up * 0.5
    o_ref[...] += jnp.dot(act.astype(BF16), wd_ref[...], preferred_element_type=F32)


def _ffn(x, norm_g, w_gate, w_up, w_down, layer):
    t, d = x.shape
    f = w_gate.shape[-1]
    tm, tf = FFN_TM, FFN_TF
    return pl.pallas_call(
        _ffn_kernel,
        grid=(t // tm, f // tf),
        in_specs=[
            pl.BlockSpec((tm, d), lambda i, j: (i, 0)),
            pl.BlockSpec((None, 1, d), lambda i, j: (layer, 0, 0)),
            pl.BlockSpec((None, d, tf), lambda i, j: (layer, 0, j)),
            pl.BlockSpec((None, d, tf), lambda i, j: (layer, 0, j)),
            pl.BlockSpec((None, tf, d), lambda i, j: (layer, j, 0)),
        ],
        out_specs=pl.BlockSpec((tm, d), lambda i, j: (i, 0)),
        out_shape=jax.ShapeDtypeStruct((t, d), F32),
        scratch_shapes=[pltpu.VMEM((tm, d), BF16)],
        compiler_params=_params("parallel", "arbitrary"),
        name="ffn",
    )(x, norm_g, w_gate, w_up, w_down)


def _pw1_kernel(x_ref, g_ref, wa_ref, wg_ref, ba_ref, bg_ref, o_ref, h_ref):
    @pl.when(pl.program_id(1) == 0)
    def _():
        h_ref[...] = _rms_norm_rows(x_ref[...], g_ref[...]).astype(BF16)

    h = h_ref[...]
    a = jnp.dot(h, wa_ref[...], preferred_element_type=F32) + ba_ref[...]
    g = jnp.dot(h, wg_ref[...], preferred_element_type=F32) + bg_ref[...]
    o_ref[...] = a * _sigmoid(g)


def _pw1_glu(x, norm_g, w, b, layer, conv_layer):
    t, d = x.shape
    tm, tn = PROJ_TM, PROJ_TN
    nj = d // tn
    return pl.pallas_call(
        _pw1_kernel,
        grid=(t // tm, nj),
        in_specs=[
            pl.BlockSpec((tm, d), lambda i, j: (i, 0)),
            pl.BlockSpec((None, 1, d), lambda i, j: (layer, 0, 0)),
            pl.BlockSpec((None, d, tn), lambda i, j: (conv_layer, 0, j)),
            pl.BlockSpec((None, d, tn), lambda i, j: (conv_layer, 0, j + nj)),
            pl.BlockSpec((None, 1, tn), lambda i, j: (conv_layer, 0, j)),
            pl.BlockSpec((None, 1, tn), lambda i, j: (conv_layer, 0, j + nj)),
        ],
        out_specs=pl.BlockSpec((tm, tn), lambda i, j: (i, j)),
        out_shape=jax.ShapeDtypeStruct((t, d), F32),
        scratch_shapes=[pltpu.VMEM((tm, d), BF16)],
        compiler_params=_params("parallel", "arbitrary"),
        name="conv_pw1_glu",
    )(x, norm_g, w, w, b, b)


def _conv_kernel(u_ref, halo_ref, dww_ref, dwb_ref, lng_ref, lnb_ref, w2_ref,
                 b2_ref, x_ref, o_ref, ext_ref, cv_ref, y_ref):
    ts, d = u_ref.shape
    n_slabs = d // LANES
    pad = CONV_HALO - (CONV_WIDTH - 1)
    first = pl.program_id(1) == 0

    for c in range(n_slabs):
        lanes = slice(c * LANES, (c + 1) * LANES)
        halo = halo_ref[:, lanes]
        ext_ref[c, 0:CONV_HALO, :] = jnp.where(first, 0.0, halo)
        ext_ref[c, CONV_HALO:, :] = u_ref[:, lanes]

    def slab_body(c, carry):
        w_rows = [dww_ref[c, k:k + 1, :] for k in range(CONV_WIDTH)]
        bias = dwb_ref[c]

        def row_body(r, carry2):
            r0 = pl.multiple_of(r * CONV_ROWS, CONV_ROWS)
            acc = jnp.broadcast_to(bias, (CONV_ROWS, LANES))
            for k in range(CONV_WIDTH):
                acc = acc + w_rows[k] * ext_ref[c, pl.ds(r0 + pad + k, CONV_ROWS), :]
            cv_ref[c, pl.ds(r0, CONV_ROWS), :] = acc
            return carry2

        return lax.fori_loop(0, ts // CONV_ROWS, row_body, carry)

    lax.fori_loop(0, n_slabs, slab_body, 0)

    total = cv_ref[0]
    for c in range(1, n_slabs):
        total = total + cv_ref[c]
    mu = jnp.sum(total, axis=-1, keepdims=True) * (1.0 / d)
    sq = jnp.zeros((ts, LANES), F32)
    for c in range(n_slabs):
        xc = cv_ref[c] - mu
        sq = sq + xc * xc
    rstd = lax.rsqrt(jnp.sum(sq, axis=-1, keepdims=True) * (1.0 / d) + NORM_EPS)
    for c in range(n_slabs):
        lanes = slice(c * LANES, (c + 1) * LANES)
        y = (cv_ref[c] - mu) * rstd * lng_ref[:, lanes] + lnb_ref[:, lanes]
        y_ref[:, lanes] = (y * _sigmoid(y)).astype(BF16)

    o_ref[...] = (x_ref[...] + b2_ref[...]
                  + jnp.dot(y_ref[...], w2_ref[...], preferred_element_type=F32))


def _conv_module(u, x, dw_w, dw_b, ln_g, ln_b, w2, b2, conv_layer, batch):
    t, d = x.shape
    s = t // batch
    ts = CONV_TS
    n_slabs = d // LANES
    spb = s // ts
    hpt = ts // CONV_HALO

    def halo_map(b, i):
        return (jnp.maximum((b * spb + i) * hpt - 1, 0), 0)

    row = lambda b, i: (b * spb + i, 0)
    vec = lambda b, i: (conv_layer, 0, 0)
    return pl.pallas_call(
        _conv_kernel,
        grid=(batch, spb),
        in_specs=[
            pl.BlockSpec((ts, d), row),
            pl.BlockSpec((CONV_HALO, d), halo_map),
            pl.BlockSpec((None, n_slabs, CONV_WIDTH, LANES), lambda b, i: (conv_layer, 0, 0, 0)),
            pl.BlockSpec((None, n_slabs, 1, LANES), lambda b, i: (conv_layer, 0, 0, 0)),
            pl.BlockSpec((None, 1, d), vec),
            pl.BlockSpec((None, 1, d), vec),
            pl.BlockSpec((None, d, d), vec),
            pl.BlockSpec((None, 1, d), vec),
            pl.BlockSpec((ts, d), row),
        ],
        out_specs=pl.BlockSpec((ts, d), row),
        out_shape=jax.ShapeDtypeStruct((t, d), F32),
        scratch_shapes=[
            pltpu.VMEM((n_slabs, ts + CONV_HALO, LANES), F32),
            pltpu.VMEM((n_slabs, ts, LANES), F32),
            pltpu.VMEM((ts, d), BF16),
        ],
        compiler_params=_params("parallel", "arbitrary"),
        name="conv_dw_ln_pw2",
    )(u, u, dw_w, dw_b, ln_g, ln_b, w2, b2, x)


def _qkv_kernel(x_ref, g_ref, w_ref, qn_ref, kn_ref, o_ref, h_ref):
    j = pl.program_id(1)
    heads_per_step = o_ref.shape[0]
    n_q_steps = N_HEADS // heads_per_step

    @pl.when(j == 0)
    def _():
        h_ref[...] = _rms_norm_rows(x_ref[...], g_ref[...]).astype(BF16)

    y = jnp.dot(h_ref[...], w_ref[...], preferred_element_type=F32)
    is_q = j < n_q_steps
    is_v = j >= 2 * n_q_steps
    gain = jnp.where(is_q, qn_ref[...] * (HEAD_DIM ** -0.5 * LOG2E), kn_ref[...])
    for c in range(heads_per_step):
        t = y[:, c * HEAD_DIM:(c + 1) * HEAD_DIM]
        normed = _rms_norm_rows(t, gain)
        o_ref[c] = jnp.where(is_v, t, normed).astype(BF16)


def _qkv_proj(x, norm_g, w, q_norm, k_norm, layer, attn_layer):
    t, d = x.shape
    tm, tn = PROJ_TM, PROJ_TN
    hps = tn // HEAD_DIM
    return pl.pallas_call(
        _qkv_kernel,
        grid=(t // tm, 3 * d // tn),
        in_specs=[
            pl.BlockSpec((tm, d), lambda i, j: (i, 0)),
            pl.BlockSpec((None, 1, d), lambda i, j: (layer, 0, 0)),
            pl.BlockSpec((None, d, tn), lambda i, j: (attn_layer, 0, j)),
            pl.BlockSpec((None, 1, HEAD_DIM), lambda i, j: (attn_layer, 0, 0)),
            pl.BlockSpec((None, 1, HEAD_DIM), lambda i, j: (attn_layer, 0, 0)),
        ],
        out_specs=pl.BlockSpec((hps, tm, HEAD_DIM), lambda i, j: (j, i, 0)),
        out_shape=jax.ShapeDtypeStruct((3 * N_HEADS, t, HEAD_DIM), BF16),
        scratch_shapes=[pltpu.VMEM((tm, d), BF16)],
        compiler_params=_params("parallel", "arbitrary"),
        name="attn_qkv",
    )(x, norm_g, w, q_norm, k_norm)


def _t5_bucket(dist):
    max_exact = NUM_BUCKETS // 2
    n = jnp.maximum(dist, 0)
    nf = jnp.maximum(n, 1).astype(F32)
    large = max_exact + (jnp.log(nf / max_exact) / math.log(MAX_DISTANCE / max_exact)
                         * (NUM_BUCKETS - max_exact)).astype(jnp.int32)
    large = jnp.minimum(large, NUM_BUCKETS - 1)
    return jnp.where(n < max_exact, n, large)


def _bias_tile_cap(n_blocks):
    max_exact = NUM_BUCKETS // 2
    d = max_exact
    while max_exact + int(math.log(d / max_exact) / math.log(MAX_DISTANCE / max_exact)
                          * (NUM_BUCKETS - max_exact)) < NUM_BUCKETS - 1:
        d += 1
    cap = -(-(d + MOBA_BLOCK - 1) // MOBA_BLOCK)
    return min(cap, n_blocks - 1)


def _attn_kernel(tbl_ref, bucket_ref, q_ref, k_ref, v_ref, o_ref,
                 bias_ref, kx_ref, qx_ref, vt_ref):
    hg = pl.program_id(0)
    b = pl.program_id(1)
    i = pl.program_id(2)
    blk = MOBA_BLOCK
    n_heads, s_len, hd = q_ref.shape
    nb = s_len // blk
    chunk = ATTN_CHUNK_BLOCKS
    cw = chunk * blk
    n_tiles = bias_ref.shape[1]
    cap = n_tiles - (2 * chunk - 1)
    nt_dims = (((1,), (1,)), ((), ()))

    @pl.when((b == 0) & (i == 0))
    def _():
        x_idx = lax.broadcasted_iota(jnp.int32, (blk, blk), 0)
        y_idx = lax.broadcasted_iota(jnp.int32, (blk, blk), 1)
        for g in range(n_heads):
            h = hg * n_heads + g
            for j in range(n_tiles):
                dl = min(cap + chunk - 1 - j, cap)
                if dl < 0:
                    bias_ref[g, j] = jnp.full((blk, blk), NEG_INF, F32)
                    continue
                bk = bucket_ref[j:j + 1, :]
                w = jnp.zeros(bk.shape, F32)
                for bucket in range(NUM_BUCKETS):
                    w = jnp.where(bk == bucket, tbl_ref[h, bucket], w)
                rows = jnp.broadcast_to(w * LOG2E, (blk, 2 * blk))
                tile = pltpu.roll(rows, blk + 1, 1, stride=1, stride_axis=0)[:, :blk]
                if dl == 0:
                    tile = jnp.where(x_idx <= y_idx, tile, NEG_INF)
                bias_ref[g, j] = tile

    @pl.when(i == 0)
    def _():
        for g in range(n_heads):
            q = q_ref[g]
            k = k_ref[g]
            kmean = jnp.sum(k.astype(F32).reshape(nb, blk, hd), axis=1) * (1.0 / blk)
            km_hi = kmean.astype(BF16)
            km_lo = (kmean - km_hi.astype(F32)).astype(BF16)
            gate = (lax.dot_general(km_hi, q, nt_dims, preferred_element_type=F32)
                    + lax.dot_general(km_lo, q, nt_dims, preferred_element_type=F32))
            n_idx = lax.broadcasted_iota(jnp.int32, gate.shape, 0)
            own = lax.shift_right_logical(
                lax.broadcasted_iota(jnp.int32, gate.shape, 1), int(math.log2(blk)))
            cnt = jnp.zeros(gate.shape, jnp.int32)
            for n in range(nb):
                row = gate[n:n + 1, :]
                beats = jnp.where(row > gate, 1, jnp.where((row == gate) & (n < n_idx), 1, 0))
                cnt = cnt + jnp.where(n < own, beats, 0)
            keep = ((n_idx < own) & (cnt < MOBA_TOPK)) | (n_idx == own)
            pen = jnp.where(keep, 0.0, NEG_INF)
            pen = jnp.concatenate([pen, jnp.zeros((hd - nb, s_len), F32)], axis=0)
            qx_ref[g, :, :hd] = q
            qx_ref[g, :, hd:] = pen.T.astype(BF16)
            row_blk = lax.shift_right_logical(
                lax.broadcasted_iota(jnp.int32, (s_len, hd), 0), int(math.log2(blk)))
            lane = lax.broadcasted_iota(jnp.int32, (s_len, hd), 1)
            kx_ref[g, :, :hd] = k
            kx_ref[g, :, hd:] = jnp.where(row_blk == lane, 1.0, 0.0).astype(BF16)
            v = v_ref[g].astype(F32)
            for c in range(s_len // cw):
                vt_ref[g, c] = v[c * cw:(c + 1) * cw, :].T.astype(BF16)

    q0 = pl.multiple_of(i * blk, blk)

    def scores(g, c):
        k0 = pl.multiple_of(c * cw, cw)
        s = lax.dot_general(kx_ref[g, pl.ds(k0, cw), :], qx_ref[g, pl.ds(q0, blk), :],
                            nt_dims, preferred_element_type=F32)
        j0 = jnp.maximum(cap + (chunk - 1) - (i - c * chunk), 0)
        s = s + bias_ref[g, pl.ds(j0, chunk)].reshape(cw, blk)
        return s, jnp.max(s, axis=0, keepdims=True)

    def softmax_step(g, c, s, s_max, m, l, acc):
        m_new = jnp.maximum(m, s_max)
        alpha = jnp.exp2(m - m_new)
        p = jnp.exp2(s - m_new)
        l = alpha * l + jnp.sum(p, axis=0, keepdims=True)
        acc = alpha * acc + jnp.dot(vt_ref[g, c], p.astype(BF16), preferred_element_type=F32)
        return m_new, l, acc

    last = i // chunk
    state = []
    for g in range(n_heads):
        state += [*scores(g, 0), jnp.full((1, blk), NEG_INF, F32), jnp.zeros((1, blk), F32),
                  jnp.zeros((hd, blk), F32)]

    def body(c, carry):
        out = []
        for g in range(n_heads):
            s, s_max, m, l, acc = carry[5 * g:5 * g + 5]
            nxt = scores(g, c + 1)
            out += [*nxt, *softmax_step(g, c, s, s_max, m, l, acc)]
        return tuple(out)

    state = lax.fori_loop(0, last, body, tuple(state))
    for g in range(n_heads):
        m, l, acc = softmax_step(g, last, *state[5 * g:5 * g + 5])
        o_ref[:, g * hd:(g + 1) * hd] = (acc * (1.0 / l)).T.astype(BF16)


def _moba_attention(qkv, rel_bias, batch):
    _, t, hd = qkv.shape
    s = t // batch
    blk = MOBA_BLOCK
    nb = s // blk
    g, chunk = ATTN_HEADS_PER_STEP, ATTN_CHUNK_BLOCKS
    groups = N_HEADS // g
    cap = _bias_tile_cap(nb)
    n_tiles = cap + 2 * chunk - 1
    dl = jnp.minimum(cap + (chunk - 1) - jnp.arange(n_tiles, dtype=jnp.int32), cap)[:, None]
    n = jnp.arange(2 * blk, dtype=jnp.int32)[None, :]
    buckets = _t5_bucket(dl * blk + n - (blk - 1))
    once = dict(pipeline_mode=pl.Buffered(1))
    return pl.pallas_call(
        _attn_kernel,
        grid=(groups, batch, nb),
        in_specs=[
            pl.BlockSpec(memory_space=pltpu.SMEM),
            pl.BlockSpec((n_tiles, 2 * blk), lambda hg, b, i: (0, 0)),
            pl.BlockSpec((g, s, hd), lambda hg, b, i: (hg, b, 0), **once),
            pl.BlockSpec((g, s, hd), lambda hg, b, i: (groups + hg, b, 0), **once),
            pl.BlockSpec((g, s, hd), lambda hg, b, i: (2 * groups + hg, b, 0), **once),
        ],
        out_specs=pl.BlockSpec((blk, g * hd), lambda hg, b, i: (b * nb + i, hg)),
        out_shape=jax.ShapeDtypeStruct((t, N_HEADS * hd), BF16),
        scratch_shapes=[
            pltpu.VMEM((g, n_tiles, blk, blk), F32),
            pltpu.VMEM((g, s, 2 * hd), BF16),
            pltpu.VMEM((g, s, 2 * hd), BF16),
            pltpu.VMEM((g, s // (chunk * blk), hd, chunk * blk), BF16),
        ],
        compiler_params=_params("arbitrary", "arbitrary", "arbitrary"),
        name="moba_attention",
    )(rel_bias.T.astype(F32), buckets, qkv, qkv, qkv)


def _wo_kernel(a_ref, w_ref, x_ref, o_ref):
    o_ref[...] = x_ref[...] + jnp.dot(a_ref[...], w_ref[...], preferred_element_type=F32)


def _wo_proj(a, x, w, attn_layer):
    t, d = x.shape
    tm = OUT_TM
    return pl.pallas_call(
        _wo_kernel,
        grid=(t // tm,),
        in_specs=[
            pl.BlockSpec((tm, d), lambda i: (i, 0)),
            pl.BlockSpec((None, d, d), lambda i: (attn_layer, 0, 0)),
            pl.BlockSpec((tm, d), lambda i: (i, 0)),
        ],
        out_specs=pl.BlockSpec((tm, d), lambda i: (i, 0)),
        out_shape=jax.ShapeDtypeStruct((t, d), F32),
        compiler_params=_params("parallel"),
        name="attn_wo",
    )(a, w, x)


def kernel(x, rel_bias, ffn1_norm, ffn1_w_gate, ffn1_w_up, ffn1_w_down, mix_norm, ffn2_norm, ffn2_w_gate, ffn2_w_up, ffn2_w_down, conv_pw1_w, conv_pw1_b, conv_dw_w, conv_dw_b, conv_ln_g, conv_ln_b, conv_pw2_w, conv_pw2_b, attn_wqkv, attn_q_norm, attn_k_norm, attn_wo):
    batch, seq, d = x.shape
    depth = ffn1_norm.shape[0]
    n_slabs = d // LANES
    bf = lambda w: w.astype(BF16)
    vec = lambda v: v.reshape(v.shape[0], 1, v.shape[-1])
    ffn1 = (vec(ffn1_norm), bf(ffn1_w_gate), bf(ffn1_w_up), bf(ffn1_w_down))
    ffn2 = (vec(ffn2_norm), bf(ffn2_w_gate), bf(ffn2_w_up), bf(ffn2_w_down))
    mix_g = vec(mix_norm)
    pw1_w, pw1_b = bf(conv_pw1_w), vec(conv_pw1_b)
    nc = conv_dw_w.shape[0]
    dw_w = conv_dw_w.reshape(nc, CONV_WIDTH, n_slabs, LANES).transpose(0, 2, 1, 3)
    dw_b = conv_dw_b.reshape(nc, n_slabs, 1, LANES)
    pw2_w = bf(conv_pw2_w)
    wqkv, wo = bf(attn_wqkv), bf(attn_wo)

    xt = x.reshape(batch * seq, d)
    for i in range(depth):
        xt = _ffn(xt, *ffn1, i)
        j = i // N_MIXERS
        if i % N_MIXERS == 0:
            u = _pw1_glu(xt, mix_g, pw1_w, pw1_b, i, j)
            xt = _conv_module(u, xt, dw_w, dw_b, vec(conv_ln_g), vec(conv_ln_b),
                              pw2_w, vec(conv_pw2_b), j, batch)
        else:
            qkv = _qkv_proj(xt, mix_g, wqkv, vec(attn_q_norm), vec(attn_k_norm), i, j)
            a = _moba_attention(qkv, rel_bias, batch)
            xt = _wo_proj(a, xt, wo, j)
        xt = _ffn(xt, *ffn2, i)
    return xt.reshape(batch, seq, d)
```

```python
import functools
import math

import jax
import jax.numpy as jnp
from jax import lax
from jax.experimental import pallas as pl
from jax.experimental.pallas import tpu as pltpu

D_MODEL = 2048
N_HEADS = 16
HEAD_DIM = D_MODEL // N_HEADS
D_FF = 5632
CONV_WIDTH = 31
MOBA_BLOCK = 256
MOBA_TOPK = 3
NUM_BUCKETS = 32
MAX_DISTANCE = 2048
NORM_EPS = 1e-6
NEG_INF = -1e30
N_MIXERS = 2

LANES = 128
SUBLANES = 8
VMEM_LIMIT_BYTES = 60000 * 1024

BF16 = jnp.bfloat16
F32 = jnp.float32

FFN_TM = 1024
FFN_TF = 256
PROJ_TM = 1024
PROJ_TN = 512
OUT_TM = 512
CONV_TS = 256
CONV_HALO = 32
CONV_ROWS = 64
ATTN_HEADS_PER_STEP = 4
ATTN_CHUNK_BLOCKS = 2
LOG2E = math.log2(math.e)


def _params(*semantics):
    return pltpu.CompilerParams(
        dimension_semantics=semantics, vmem_limit_bytes=VMEM_LIMIT_BYTES)


def _rms_norm_rows(x, g):
    ms = jnp.mean(x * x, axis=-1, keepdims=True)
    return (x * lax.rsqrt(ms + NORM_EPS)) * g


def _sigmoid(x):
    return 1.0 / (1.0 + jnp.exp(-x))


def _ffn_kernel(x_hbm, g_ref, wg_ref, wu_ref, wd_ref, o_ref, x_buf, h_ref, x_sem):
    i = pl.program_id(0)
    f = pl.program_id(1)
    tm = x_buf.shape[0]

    def x_copy(tile):
        rows = pl.ds(pl.multiple_of(tile * tm, tm), tm)
        return pltpu.make_async_copy(x_hbm.at[rows, :], x_buf, x_sem)

    @pl.when((i == 0) & (f == 0))
    def _():
        x_copy(0).start()

    @pl.when(f == 0)
    def _():
        x_copy(i).wait()
        x = x_buf[...]
        h_ref[...] = _rms_norm_rows(x, g_ref[...]).astype(BF16)
        o_ref[...] = x

    @pl.when((f == 1) & (i + 1 < pl.num_programs(0)))
    def _():
        x_copy(i + 1).start()

    h = h_ref[...]
    gate = jnp.dot(h, wg_ref[...].astype(BF16), preferred_element_type=F32)
    up = jnp.dot(h, wu_ref[...].astype(BF16), preferred_element_type=F32)
    act = (gate * _sigmoid(gate)) * up * 0.5
    o_ref[...] += jnp.dot(act.astype(BF16), wd_ref[...].astype(BF16),
                          preferred_element_type=F32)


def _ffn(x, norm_g, w_gate, w_up, w_down, layer):
    t, d = x.shape
    f = w_gate.shape[-1]
    tm, tf = FFN_TM, FFN_TF
    assert f // tf >= 2
    return pl.pallas_call(
        _ffn_kernel,
        grid=(t // tm, f // tf),
        in_specs=[
            pl.BlockSpec(memory_space=pl.ANY),
            pl.BlockSpec((None, 1, d), lambda i, j: (layer, 0, 0)),
            pl.BlockSpec((None, d, tf), lambda i, j: (layer, 0, j)),
            pl.BlockSpec((None, d, tf), lambda i, j: (layer, 0, j)),
            pl.BlockSpec((None, tf, d), lambda i, j: (layer, j, 0)),
        ],
        out_specs=pl.BlockSpec((tm, d), lambda i, j: (i, 0)),
        out_shape=jax.ShapeDtypeStruct((t, d), F32),
        scratch_shapes=[
            pltpu.VMEM((tm, d), F32),
            pltpu.VMEM((tm, d), BF16),
            pltpu.SemaphoreType.DMA(()),
        ],
        compiler_params=_params("arbitrary", "arbitrary"),
        name="ffn",
    )(x, norm_g, w_gate, w_up, w_down)


def _pw1_kernel(x_ref, g_ref, wa_ref, wg_ref, ba_ref, bg_ref, o_ref, h_ref):
    @pl.when(pl.program_id(1) == 0)
    def _():
        h_ref[...] = _rms_norm_rows(x_ref[...], g_ref[...]).astype(BF16)

    h = h_ref[...]
    a = jnp.dot(h, wa_ref[...].astype(BF16), preferred_element_type=F32) + ba_ref[...]
    g = jnp.dot(h, wg_ref[...].astype(BF16), preferred_element_type=F32) + bg_ref[...]
    o_ref[...] = a * _sigmoid(g)


def _pw1_glu(x, norm_g, w, b, layer, conv_layer):
    t, d = x.shape
    tm, tn = PROJ_TM, PROJ_TN
    nj = d // tn
    return pl.pallas_call(
        _pw1_kernel,
        grid=(t // tm, nj),
        in_specs=[
            pl.BlockSpec((tm, d), lambda i, j: (i, 0)),
            pl.BlockSpec((None, 1, d), lambda i, j: (layer, 0, 0)),
            pl.BlockSpec((None, d, tn), lambda i, j: (conv_layer, 0, j)),
            pl.BlockSpec((None, d, tn), lambda i, j: (conv_layer, 0, j + nj)),
            pl.BlockSpec((None, 1, tn), lambda i, j: (conv_layer, 0, j)),
            pl.BlockSpec((None, 1, tn), lambda i, j: (conv_layer, 0, j + nj)),
        ],
        out_specs=pl.BlockSpec((tm, tn), lambda i, j: (i, j)),
        out_shape=jax.ShapeDtypeStruct((t, d), F32),
        scratch_shapes=[pltpu.VMEM((tm, d), BF16)],
        compiler_params=_params("parallel", "arbitrary"),
        name="conv_pw1_glu",
    )(x, norm_g, w, w, b, b)


def _conv_kernel(u_ref, halo_ref, dww_ref, dwb_ref, lng_ref, lnb_ref, w2_ref,
                 b2_ref, x_ref, o_ref, ext_ref, cv_ref, y_ref):
    ts, d = u_ref.shape
    n_slabs = d // LANES
    pad = CONV_HALO - (CONV_WIDTH - 1)
    first = pl.program_id(1) == 0

    for c in range(n_slabs):
        lanes = slice(c * LANES, (c + 1) * LANES)
        halo = halo_ref[:, lanes]
        ext_ref[c, 0:CONV_HALO, :] = jnp.where(first, 0.0, halo)
        ext_ref[c, CONV_HALO:, :] = u_ref[:, lanes]

    def slab_body(c, carry):
        w_rows = [dww_ref[c, k:k + 1, :] for k in range(CONV_WIDTH)]
        bias = dwb_ref[c]

        def row_body(r, carry2):
            r0 = pl.multiple_of(r * CONV_ROWS, CONV_ROWS)
            acc = jnp.broadcast_to(bias, (CONV_ROWS, LANES))
            for k in range(CONV_WIDTH):
                acc = acc + w_rows[k] * ext_ref[c, pl.ds(r0 + pad + k, CONV_ROWS), :]
            cv_ref[c, pl.ds(r0, CONV_ROWS), :] = acc
            return carry2

        return lax.fori_loop(0, ts // CONV_ROWS, row_body, carry)

    lax.fori_loop(0, n_slabs, slab_body, 0)

    total = cv_ref[0]
    for c in range(1, n_slabs):
        total = total + cv_ref[c]
    mu = jnp.sum(total, axis=-1, keepdims=True) * (1.0 / d)
    sq = jnp.zeros((ts, LANES), F32)
    for c in range(n_slabs):
        xc = cv_ref[c] - mu
        sq = sq + xc * xc
    rstd = lax.rsqrt(jnp.sum(sq, axis=-1, keepdims=True) * (1.0 / d) + NORM_EPS)
    for c in range(n_slabs):
        lanes = slice(c * LANES, (c + 1) * LANES)
        y = (cv_ref[c] - mu) * rstd * lng_ref[:, lanes] + lnb_ref[:, lanes]
        y_ref[:, lanes] = (y * _sigmoid(y)).astype(BF16)

    o_ref[...] = (x_ref[...] + b2_ref[...]
                  + jnp.dot(y_ref[...], w2_ref[...], preferred_element_type=F32))


def _conv_module(u, x, dw_w, dw_b, ln_g, ln_b, w2, b2, conv_layer, batch):
    t, d = x.shape
    s = t // batch
    ts = CONV_TS
    n_slabs = d // LANES
    spb = s // ts
    hpt = ts // CONV_HALO

    def halo_map(b, i):
        return (jnp.maximum((b * spb + i) * hpt - 1, 0), 0)

    row = lambda b, i: (b * spb + i, 0)
    vec = lambda b, i: (conv_layer, 0, 0)
    return pl.pallas_call(
        _conv_kernel,
        grid=(batch, spb),
        in_specs=[
            pl.BlockSpec((ts, d), row),
            pl.BlockSpec((CONV_HALO, d), halo_map),
            pl.BlockSpec((None, n_slabs, CONV_WIDTH, LANES), lambda b, i: (conv_layer, 0, 0, 0)),
            pl.BlockSpec((None, n_slabs, 1, LANES), lambda b, i: (conv_layer, 0, 0, 0)),
            pl.BlockSpec((None, 1, d), vec),
            pl.BlockSpec((None, 1, d), vec),
            pl.BlockSpec((None, d, d), vec),
            pl.BlockSpec((None, 1, d), vec),
            pl.BlockSpec((ts, d), row),
        ],
        out_specs=pl.BlockSpec((ts, d), row),
        out_shape=jax.ShapeDtypeStruct((t, d), F32),
        scratch_shapes=[
            pltpu.VMEM((n_slabs, ts + CONV_HALO, LANES), F32),
            pltpu.VMEM((n_slabs, ts, LANES), F32),
            pltpu.VMEM((ts, d), BF16),
        ],
        compiler_params=_params("parallel", "arbitrary"),
        name="conv_dw_ln_pw2",
    )(u, u, dw_w, dw_b, ln_g, ln_b, w2, b2, x)


def _qkv_kernel(x_ref, g_ref, w_ref, qn_ref, kn_ref, o_ref, h_ref):
    j = pl.program_id(1)
    heads_per_step = o_ref.shape[0]
    n_q_steps = N_HEADS // heads_per_step

    @pl.when(j == 0)
    def _():
        h_ref[...] = _rms_norm_rows(x_ref[...], g_ref[...]).astype(BF16)

    y = jnp.dot(h_ref[...], w_ref[...].astype(BF16), preferred_element_type=F32)
    is_q = j < n_q_steps
    is_v = j >= 2 * n_q_steps
    gain = jnp.where(is_q, qn_ref[...] * (HEAD_DIM ** -0.5 * LOG2E), kn_ref[...])
    for c in range(heads_per_step):
        t = y[:, c * HEAD_DIM:(c + 1) * HEAD_DIM]
        normed = _rms_norm_rows(t, gain)
        o_ref[c] = jnp.where(is_v, t, normed).astype(BF16)


def _qkv_proj(x, norm_g, w, q_norm, k_norm, layer, attn_layer):
    t, d = x.shape
    tm, tn = PROJ_TM, PROJ_TN
    hps = tn // HEAD_DIM
    return pl.pallas_call(
        _qkv_kernel,
        grid=(t // tm, 3 * d // tn),
        in_specs=[
            pl.BlockSpec((tm, d), lambda i, j: (i, 0)),
            pl.BlockSpec((None, 1, d), lambda i, j: (layer, 0, 0)),
            pl.BlockSpec((None, d, tn), lambda i, j: (attn_layer, 0, j)),
            pl.BlockSpec((None, 1, HEAD_DIM), lambda i, j: (attn_layer, 0, 0)),
            pl.BlockSpec((None, 1, HEAD_DIM), lambda i, j: (attn_layer, 0, 0)),
        ],
        out_specs=pl.BlockSpec((hps, tm, HEAD_DIM), lambda i, j: (j, i, 0)),
        out_shape=jax.ShapeDtypeStruct((3 * N_HEADS, t, HEAD_DIM), BF16),
        scratch_shapes=[pltpu.VMEM((tm, d), BF16)],
        compiler_params=_params("parallel", "arbitrary"),
        name="attn_qkv",
    )(x, norm_g, w, q_norm, k_norm)


def _t5_bucket(dist):
    max_exact = NUM_BUCKETS // 2
    n = jnp.maximum(dist, 0)
    nf = jnp.maximum(n, 1).astype(F32)
    large = max_exact + (jnp.log(nf / max_exact) / math.log(MAX_DISTANCE / max_exact)
                         * (NUM_BUCKETS - max_exact)).astype(jnp.int32)
    large = jnp.minimum(large, NUM_BUCKETS - 1)
    return jnp.where(n < max_exact, n, large)


def _bias_tile_cap(n_blocks):
    max_exact = NUM_BUCKETS // 2
    d = max_exact
    while max_exact + int(math.log(d / max_exact) / math.log(MAX_DISTANCE / max_exact)
                          * (NUM_BUCKETS - max_exact)) < NUM_BUCKETS - 1:
        d += 1
    cap = -(-(d + MOBA_BLOCK - 1) // MOBA_BLOCK)
    return min(cap, n_blocks - 1)


def _attn_kernel(tbl_ref, bucket_ref, q_ref, k_ref, v_ref, o_ref,
                 bias_ref, kx_ref, qx_ref, vt_ref):
    hg = pl.program_id(0)
    b = pl.program_id(1)
    i = pl.program_id(2)
    blk = MOBA_BLOCK
    n_heads, s_len, hd = q_ref.shape
    nb = s_len // blk
    chunk = ATTN_CHUNK_BLOCKS
    cw = chunk * blk
    n_tiles = bias_ref.shape[1]
    cap = n_tiles - (2 * chunk - 1)
    nt_dims = (((1,), (1,)), ((), ()))

    @pl.when((b == 0) & (i == 0))
    def _():
        x_idx = lax.broadcasted_iota(jnp.int32, (blk, blk), 0)
        y_idx = lax.broadcasted_iota(jnp.int32, (blk, blk), 1)
        for g in range(n_heads):
            h = hg * n_heads + g
            for j in range(n_tiles):
                dl = min(cap + chunk - 1 - j, cap)
                if dl < 0:
                    bias_ref[g, j] = jnp.full((blk, blk), NEG_INF, F32)
                    continue
                bk = bucket_ref[j:j + 1, :]
                w = jnp.zeros(bk.shape, F32)
                for bucket in range(NUM_BUCKETS):
                    w = jnp.where(bk == bucket, tbl_ref[h, bucket], w)
                rows = jnp.broadcast_to(w * LOG2E, (blk, 2 * blk))
                tile = pltpu.roll(rows, blk + 1, 1, stride=1, stride_axis=0)[:, :blk]
                if dl == 0:
                    tile = jnp.where(x_idx <= y_idx, tile, NEG_INF)
                bias_ref[g, j] = tile

    @pl.when(i == 0)
    def _():
        for g in range(n_heads):
            q = q_ref[g]
            k = k_ref[g]
            kmean = jnp.sum(k.astype(F32).reshape(nb, blk, hd), axis=1) * (1.0 / blk)
            km_hi = kmean.astype(BF16)
            km_lo = (kmean - km_hi.astype(F32)).astype(BF16)
            gate = (lax.dot_general(km_hi, q, nt_dims, preferred_element_type=F32)
                    + lax.dot_general(km_lo, q, nt_dims, preferred_element_type=F32))
            n_idx = lax.broadcasted_iota(jnp.int32, gate.shape, 0)
            own = lax.shift_right_logical(
                lax.broadcasted_iota(jnp.int32, gate.shape, 1), int(math.log2(blk)))
            cnt = jnp.zeros(gate.shape, jnp.int32)
            for n in range(nb):
                row = gate[n:n + 1, :]
                beats = jnp.where(row > gate, 1, jnp.where((row == gate) & (n < n_idx), 1, 0))
                cnt = cnt + jnp.where(n < own, beats, 0)
            keep = ((n_idx < own) & (cnt < MOBA_TOPK)) | (n_idx == own)
            pen = jnp.where(keep, 0.0, NEG_INF)
            pen = jnp.concatenate([pen, jnp.zeros((hd - nb, s_len), F32)], axis=0)
            qx_ref[g, :, :hd] = q
            qx_ref[g, :, hd:] = pen.T.astype(BF16)
            row_blk = lax.shift_right_logical(
                lax.broadcasted_iota(jnp.int32, (s_len, hd), 0), int(math.log2(blk)))
            lane = lax.broadcasted_iota(jnp.int32, (s_len, hd), 1)
            kx_ref[g, :, :hd] = k
            kx_ref[g, :, hd:] = jnp.where(row_blk == lane, 1.0, 0.0).astype(BF16)
            v = v_ref[g].astype(F32)
            for c in range(s_len // cw):
                vt_ref[g, c] = v[c * cw:(c + 1) * cw, :].T.astype(BF16)

    q0 = pl.multiple_of(i * blk, blk)

    def scores(g, c):
        k0 = pl.multiple_of(c * cw, cw)
        s = lax.dot_general(kx_ref[g, pl.ds(k0, cw), :], qx_ref[g, pl.ds(q0, blk), :],
                            nt_dims, preferred_element_type=F32)
        j0 = jnp.maximum(cap + (chunk - 1) - (i - c * chunk), 0)
        s = s + bias_ref[g, pl.ds(j0, chunk)].reshape(cw, blk)
        return s, jnp.max(s, axis=0, keepdims=True)

    def softmax_step(g, c, s, s_max, m, l, acc):
        m_new = jnp.maximum(m, s_max)
        alpha = jnp.exp2(m - m_new)
        p = jnp.exp2(s - m_new)
        l = alpha * l + jnp.sum(p, axis=0, keepdims=True)
        acc = alpha * acc + jnp.dot(vt_ref[g, c], p.astype(BF16), preferred_element_type=F32)
        return m_new, l, acc

    last = i // chunk
    state = []
    for g in range(n_heads):
        state += [*scores(g, 0), jnp.full((1, blk), NEG_INF, F32), jnp.zeros((1, blk), F32),
                  jnp.zeros((hd, blk), F32)]

    def body(c, carry):
        out = []
        for g in range(n_heads):
            s, s_max, m, l, acc = carry[5 * g:5 * g + 5]
            nxt = scores(g, c + 1)
            out += [*nxt, *softmax_step(g, c, s, s_max, m, l, acc)]
        return tuple(out)

    state = lax.fori_loop(0, last, body, tuple(state))
    for g in range(n_heads):
        m, l, acc = softmax_step(g, last, *state[5 * g:5 * g + 5])
        o_ref[:, g * hd:(g + 1) * hd] = (acc * (1.0 / l)).T.astype(BF16)


def _moba_attention(qkv, rel_bias, batch):
    _, t, hd = qkv.shape
    s = t // batch
    blk = MOBA_BLOCK
    nb = s // blk
    g, chunk = ATTN_HEADS_PER_STEP, ATTN_CHUNK_BLOCKS
    groups = N_HEADS // g
    cap = _bias_tile_cap(nb)
    n_tiles = cap + 2 * chunk - 1
    dl = jnp.minimum(cap + (chunk - 1) - jnp.arange(n_tiles, dtype=jnp.int32), cap)[:, None]
    n = jnp.arange(2 * blk, dtype=jnp.int32)[None, :]
    buckets = _t5_bucket(dl * blk + n - (blk - 1))
    once = dict(pipeline_mode=pl.Buffered(1))
    return pl.pallas_call(
        _attn_kernel,
        grid=(groups, batch, nb),
        in_specs=[
            pl.BlockSpec(memory_space=pltpu.SMEM),
            pl.BlockSpec((n_tiles, 2 * blk), lambda hg, b, i: (0, 0)),
            pl.BlockSpec((g, s, hd), lambda hg, b, i: (hg, b, 0), **once),
            pl.BlockSpec((g, s, hd), lambda hg, b, i: (groups + hg, b, 0), **once),
            pl.BlockSpec((g, s, hd), lambda hg, b, i: (2 * groups + hg, b, 0), **once),
        ],
        out_specs=pl.BlockSpec((blk, g * hd), lambda hg, b, i: (b * nb + i, hg)),
        out_shape=jax.ShapeDtypeStruct((t, N_HEADS * hd), BF16),
        scratch_shapes=[
            pltpu.VMEM((g, n_tiles, blk, blk), F32),
            pltpu.VMEM((g, s, 2 * hd), BF16),
            pltpu.VMEM((g, s, 2 * hd), BF16),
            pltpu.VMEM((g, s // (chunk * blk), hd, chunk * blk), BF16),
        ],
        compiler_params=_params("arbitrary", "arbitrary", "arbitrary"),
        name="moba_attention",
    )(rel_bias.T.astype(F32), buckets, qkv, qkv, qkv)


def _wo_kernel(a_ref, w_ref, x_ref, o_ref):
    o_ref[...] = x_ref[...] + jnp.dot(a_ref[...], w_ref[...], preferred_element_type=F32)


def _wo_proj(a, x, w, attn_layer):
    t, d = x.shape
    tm = OUT_TM
    return pl.pallas_call(
        _wo_kernel,
        grid=(t // tm,),
        in_specs=[
            pl.BlockSpec((tm, d), lambda i: (i, 0)),
            pl.BlockSpec((None, d, d), lambda i: (attn_layer, 0, 0)),
            pl.BlockSpec((tm, d), lambda i: (i, 0)),
        ],
        out_specs=pl.BlockSpec((tm, d), lambda i: (i, 0)),
        out_shape=jax.ShapeDtypeStruct((t, d), F32),
        compiler_params=_params("parallel"),
        name="attn_wo",
    )(a, w, x)


def kernel(x, rel_bias, ffn1_norm, ffn1_w_gate, ffn1_w_up, ffn1_w_down, mix_norm, ffn2_norm, ffn2_w_gate, ffn2_w_up, ffn2_w_down, conv_pw1_w, conv_pw1_b, conv_dw_w, conv_dw_b, conv_ln_g, conv_ln_b, conv_pw2_w, conv_pw2_b, attn_wqkv, attn_q_norm, attn_k_norm, attn_wo):
    batch, seq, d = x.shape
    depth = ffn1_norm.shape[0]
    n_slabs = d // LANES
    bf = lambda w: w.astype(BF16)
    vec = lambda v: v.reshape(v.shape[0], 1, v.shape[-1])
    ffn1 = (vec(ffn1_norm), ffn1_w_gate, ffn1_w_up, ffn1_w_down)
    ffn2 = (vec(ffn2_norm), ffn2_w_gate, ffn2_w_up, ffn2_w_down)
    mix_g = vec(mix_norm)
    pw1_w, pw1_b = conv_pw1_w, vec(conv_pw1_b)
    nc = conv_dw_w.shape[0]
    dw_w = conv_dw_w.reshape(nc, CONV_WIDTH, n_slabs, LANES).transpose(0, 2, 1, 3)
    dw_b = conv_dw_b.reshape(nc, n_slabs, 1, LANES)
    pw2_w = bf(conv_pw2_w)
    wqkv, wo = attn_wqkv, bf(attn_wo)

    xt = x.reshape(batch * seq, d)
    for i in range(depth):
        xt = _ffn(xt, *ffn1, i)
        j = i // N_MIXERS
        if i % N_MIXERS == 0:
            u = _pw1_glu(xt, mix_g, pw1_w, pw1_b, i, j)
            xt = _conv_module(u, xt, dw_w, dw_b, vec(conv_ln_g), vec(conv_ln_b),
                              pw2_w, vec(conv_pw2_b), j, batch)
        else:
            qkv = _qkv_proj(xt, mix_g, wqkv, vec(attn_q_norm), vec(attn_k_norm), i, j)
            a = _moba_attention(qkv, rel_bias, batch)
            xt = _wo_proj(a, xt, wo, j)
        xt = _ffn(xt, *ffn2, i)
    return xt.reshape(batch, seq, d)
```

```python
import functools
import math

import jax
import jax.numpy as jnp
from jax import lax
from jax.experimental import pallas as pl
from jax.experimental.pallas import tpu as pltpu

D_MODEL = 2048
N_HEADS = 16
HEAD_DIM = D_MODEL // N_HEADS
D_FF = 5632
CONV_WIDTH = 31
MOBA_BLOCK = 256
MOBA_TOPK = 3
NUM_BUCKETS = 32
MAX_DISTANCE = 2048
NORM_EPS = 1e-6
NEG_INF = -1e30
N_MIXERS = 2

LANES = 128
SUBLANES = 8
VMEM_LIMIT_BYTES = 60000 * 1024

BF16 = jnp.bfloat16
F32 = jnp.float32

FFN_TM = 1024
FFN_TF = 256
PROJ_TM = 1024
PROJ_TN = 512
OUT_TM = 512
CONV_TS = 256
CONV_HALO = 32
CONV_ROWS = 64
ATTN_HEADS_PER_STEP = 4
ATTN_CHUNK_BLOCKS = 2
LOG2E = math.log2(math.e)


def _params(*semantics):
    return pltpu.CompilerParams(
        dimension_semantics=semantics, vmem_limit_bytes=VMEM_LIMIT_BYTES)


def _rms_norm_rows(x, g):
    ms = jnp.mean(x * x, axis=-1, keepdims=True)
    return (x * lax.rsqrt(ms + NORM_EPS)) * g


def _sigmoid(x):
    return 1.0 / (1.0 + jnp.exp(-x))


def _ffn_kernel(x_hbm, g_ref, wg_ref, wu_ref, wd_ref, o_ref, x_buf, h_ref, x_sem):
    i = pl.program_id(0)
    f = pl.program_id(1)
    tm = x_buf.shape[0]

    def x_copy(tile):
        rows = pl.ds(pl.multiple_of(tile * tm, tm), tm)
        return pltpu.make_async_copy(x_hbm.at[rows, :], x_buf, x_sem)

    @pl.when((i == 0) & (f == 0))
    def _():
        x_copy(0).start()

    @pl.when(f == 0)
    def _():
        x_copy(i).wait()
        x = x_buf[...]
        h_ref[...] = _rms_norm_rows(x, g_ref[...]).astype(BF16)
        o_ref[...] = x

    @pl.when((f == 1) & (i + 1 < pl.num_programs(0)))
    def _():
        x_copy(i + 1).start()

    h = h_ref[...]
    gate = jnp.dot(h, wg_ref[...].astype(BF16), preferred_element_type=F32)
    up = jnp.dot(h, wu_ref[...].astype(BF16), preferred_element_type=F32)
    act = (gate * _sigmoid(gate)) * up * 0.5
    o_ref[...] += jnp.dot(act.astype(BF16), wd_ref[...].astype(BF16),
                          preferred_element_type=F32)


def _ffn(x, norm_g, w_gate, w_up, w_down, layer):
    t, d = x.shape
    f = w_gate.shape[-1]
    tm, tf = FFN_TM, FFN_TF
    assert f // tf >= 2
    return pl.pallas_call(
        _ffn_kernel,
        grid=(t // tm, f // tf),
        in_specs=[
            pl.BlockSpec(memory_space=pl.ANY),
            pl.BlockSpec((None, 1, d), lambda i, j: (layer, 0, 0)),
            pl.BlockSpec((None, d, tf), lambda i, j: (layer, 0, j)),
            pl.BlockSpec((None, d, tf), lambda i, j: (layer, 0, j)),
            pl.BlockSpec((None, tf, d), lambda i, j: (layer, j, 0)),
        ],
        out_specs=pl.BlockSpec((tm, d), lambda i, j: (i, 0)),
        out_shape=jax.ShapeDtypeStruct((t, d), F32),
        scratch_shapes=[
            pltpu.VMEM((tm, d), F32),
            pltpu.VMEM((tm, d), BF16),
            pltpu.SemaphoreType.DMA(()),
        ],
        compiler_params=_params("arbitrary", "arbitrary"),
        name="ffn",
    )(x, norm_g, w_gate, w_up, w_down)


def _pw1_kernel(x_ref, g_ref, wa_ref, wg_ref, ba_ref, bg_ref, o_ref, h_ref):
    @pl.when(pl.program_id(1) == 0)
    def _():
        h_ref[...] = _rms_norm_rows(x_ref[...], g_ref[...]).astype(BF16)

    h = h_ref[...]
    a = jnp.dot(h, wa_ref[...].astype(BF16), preferred_element_type=F32) + ba_ref[...]
    g = jnp.dot(h, wg_ref[...].astype(BF16), preferred_element_type=F32) + bg_ref[...]
    o_ref[...] = a * _sigmoid(g)


def _pw1_glu(x, norm_g, w, b, layer, conv_layer):
    t, d = x.shape
    tm, tn = PROJ_TM, PROJ_TN
    nj = d // tn
    return pl.pallas_call(
        _pw1_kernel,
        grid=(t // tm, nj),
        in_specs=[
            pl.BlockSpec((tm, d), lambda i, j: (i, 0)),
            pl.BlockSpec((None, 1, d), lambda i, j: (layer, 0, 0)),
            pl.BlockSpec((None, d, tn), lambda i, j: (conv_layer, 0, j)),
            pl.BlockSpec((None, d, tn), lambda i, j: (conv_layer, 0, j + nj)),
            pl.BlockSpec((None, 1, tn), lambda i, j: (conv_layer, 0, j)),
            pl.BlockSpec((None, 1, tn), lambda i, j: (conv_layer, 0, j + nj)),
        ],
        out_specs=pl.BlockSpec((tm, tn), lambda i, j: (i, j)),
        out_shape=jax.ShapeDtypeStruct((t, d), F32),
        scratch_shapes=[pltpu.VMEM((tm, d), BF16)],
        compiler_params=_params("parallel", "arbitrary"),
        name="conv_pw1_glu",
    )(x, norm_g, w, w, b, b)


def _conv_kernel(u_ref, halo_ref, dww_ref, dwb_ref, lng_ref, lnb_ref, w2_ref,
                 b2_ref, x_ref, o_ref, ext_ref, cv_ref, y_ref):
    ts, d = u_ref.shape
    n_slabs = d // LANES
    pad = CONV_HALO - (CONV_WIDTH - 1)
    first = pl.program_id(1) == 0

    for c in range(n_slabs):
        lanes = slice(c * LANES, (c + 1) * LANES)
        halo = halo_ref[:, lanes]
        ext_ref[c, 0:CONV_HALO, :] = jnp.where(first, 0.0, halo)
        ext_ref[c, CONV_HALO:, :] = u_ref[:, lanes]

    def slab_body(c, carry):
        w_rows = [dww_ref[c, k:k + 1, :] for k in range(CONV_WIDTH)]
        bias = dwb_ref[c]

        def row_body(r, carry2):
            r0 = pl.multiple_of(r * CONV_ROWS, CONV_ROWS)
            acc = jnp.broadcast_to(bias, (CONV_ROWS, LANES))
            for k in range(CONV_WIDTH):
                acc = acc + w_rows[k] * ext_ref[c, pl.ds(r0 + pad + k, CONV_ROWS), :]
            cv_ref[c, pl.ds(r0, CONV_ROWS), :] = acc
            return carry2

        return lax.fori_loop(0, ts // CONV_ROWS, row_body, carry)

    lax.fori_loop(0, n_slabs, slab_body, 0)

    total = cv_ref[0]
    for c in range(1, n_slabs):
        total = total + cv_ref[c]
    mu = jnp.sum(total, axis=-1, keepdims=True) * (1.0 / d)
    sq = jnp.zeros((ts, LANES), F32)
    for c in range(n_slabs):
        xc = cv_ref[c] - mu
        sq = sq + xc * xc
    rstd = lax.rsqrt(jnp.sum(sq, axis=-1, keepdims=True) * (1.0 / d) + NORM_EPS)
    for c in range(n_slabs):
        lanes = slice(c * LANES, (c + 1) * LANES)
        y = (cv_ref[c] - mu) * rstd * lng_ref[:, lanes] + lnb_ref[:, lanes]
        y_ref[:, lanes] = (y * _sigmoid(y)).astype(BF16)

    o_ref[...] = (x_ref[...] + b2_ref[...]
                  + jnp.dot(y_ref[...], w2_ref[...], preferred_element_type=F32))


def _conv_module(u, x, dw_w, dw_b, ln_g, ln_b, w2, b2, conv_layer, batch):
    t, d = x.shape
    s = t // batch
    ts = CONV_TS
    n_slabs = d // LANES
    spb = s // ts
    hpt = ts // CONV_HALO

    def halo_map(b, i):
        return (jnp.maximum((b * spb + i) * hpt - 1, 0), 0)

    row = lambda b, i: (b * spb + i, 0)
    vec = lambda b, i: (conv_layer, 0, 0)
    return pl.pallas_call(
        _conv_kernel,
        grid=(batch, spb),
        in_specs=[
            pl.BlockSpec((ts, d), row),
            pl.BlockSpec((CONV_HALO, d), halo_map),
            pl.BlockSpec((None, n_slabs, CONV_WIDTH, LANES), lambda b, i: (conv_layer, 0, 0, 0)),
            pl.BlockSpec((None, n_slabs, 1, LANES), lambda b, i: (conv_layer, 0, 0, 0)),
            pl.BlockSpec((None, 1, d), vec),
            pl.BlockSpec((None, 1, d), vec),
            pl.BlockSpec((None, d, d), vec),
            pl.BlockSpec((None, 1, d), vec),
            pl.BlockSpec((ts, d), row),
        ],
        out_specs=pl.BlockSpec((ts, d), row),
        out_shape=jax.ShapeDtypeStruct((t, d), F32),
        scratch_shapes=[
            pltpu.VMEM((n_slabs, ts + CONV_HALO, LANES), F32),
            pltpu.VMEM((n_slabs, ts, LANES), F32),
            pltpu.VMEM((ts, d), BF16),
        ],
        compiler_params=_params("parallel", "arbitrary"),
        name="conv_dw_ln_pw2",
    )(u, u, dw_w, dw_b, ln_g, ln_b, w2, b2, x)


def _qkv_kernel(x_ref, g_ref, w_ref, qn_ref, kn_ref, o_ref, h_ref):
    j = pl.program_id(1)
    heads_per_step = o_ref.shape[0]
    n_q_steps = N_HEADS // heads_per_step

    @pl.when(j == 0)
    def _():
        h_ref[...] = _rms_norm_rows(x_ref[...], g_ref[...]).astype(BF16)

    y = jnp.dot(h_ref[...], w_ref[...].astype(BF16), preferred_element_type=F32)
    is_q = j < n_q_steps
    is_v = j >= 2 * n_q_steps
    gain = jnp.where(is_q, qn_ref[...] * (HEAD_DIM ** -0.5 * LOG2E), kn_ref[...])
    for c in range(heads_per_step):
        t = y[:, c * HEAD_DIM:(c + 1) * HEAD_DIM]
        normed = _rms_norm_rows(t, gain)
        o_ref[c] = jnp.where(is_v, t, normed).astype(BF16)


def _qkv_proj(x, norm_g, w, q_norm, k_norm, layer, attn_layer):
    t, d = x.shape
    tm, tn = PROJ_TM, PROJ_TN
    hps = tn // HEAD_DIM
    return pl.pallas_call(
        _qkv_kernel,
        grid=(t // tm, 3 * d // tn),
        in_specs=[
            pl.BlockSpec((tm, d), lambda i, j: (i, 0)),
            pl.BlockSpec((None, 1, d), lambda i, j: (layer, 0, 0)),
            pl.BlockSpec((None, d, tn), lambda i, j: (attn_layer, 0, j)),
            pl.BlockSpec((None, 1, HEAD_DIM), lambda i, j: (attn_layer, 0, 0)),
            pl.BlockSpec((None, 1, HEAD_DIM), lambda i, j: (attn_layer, 0, 0)),
        ],
        out_specs=pl.BlockSpec((hps, tm, HEAD_DIM), lambda i, j: (j, i, 0)),
        out_shape=jax.ShapeDtypeStruct((3 * N_HEADS, t, HEAD_DIM), BF16),
        scratch_shapes=[pltpu.VMEM((tm, d), BF16)],
        compiler_params=_params("parallel", "arbitrary"),
        name="attn_qkv",
    )(x, norm_g, w, q_norm, k_norm)


def _t5_bucket(dist):
    max_exact = NUM_BUCKETS // 2
    n = jnp.maximum(dist, 0)
    nf = jnp.maximum(n, 1).astype(F32)
    large = max_exact + (jnp.log(nf / max_exact) / math.log(MAX_DISTANCE / max_exact)
                         * (NUM_BUCKETS - max_exact)).astype(jnp.int32)
    large = jnp.minimum(large, NUM_BUCKETS - 1)
    return jnp.where(n < max_exact, n, large)


def _bias_tile_cap(n_blocks):
    max_exact = NUM_BUCKETS // 2
    d = max_exact
    while max_exact + int(math.log(d / max_exact) / math.log(MAX_DISTANCE / max_exact)
                          * (NUM_BUCKETS - max_exact)) < NUM_BUCKETS - 1:
        d += 1
    cap = -(-(d + MOBA_BLOCK - 1) // MOBA_BLOCK)
    return min(cap, n_blocks - 1)


def _attn_kernel(tbl_ref, bucket_ref, q_ref, k_ref, v_ref, o_ref,
                 bias_ref, kx_ref, qx_ref, vt_ref, s0_ref, s1_ref, smax0_ref, smax1_ref,
                 m_ref, l_ref, acc_ref):
    hg = pl.program_id(0)
    b = pl.program_id(1)
    i = pl.program_id(2)
    blk = MOBA_BLOCK
    n_heads, s_len, hd = q_ref.shape
    nb = s_len // blk
    chunk = ATTN_CHUNK_BLOCKS
    cw = chunk * blk
    n_tiles = bias_ref.shape[1]
    cap = n_tiles - (2 * chunk - 1)
    nt_dims = (((1,), (1,)), ((), ()))

    @pl.when((b == 0) & (i == 0))
    def _():
        x_idx = lax.broadcasted_iota(jnp.int32, (blk, blk), 0)
        y_idx = lax.broadcasted_iota(jnp.int32, (blk, blk), 1)
        for g in range(n_heads):
            h = hg * n_heads + g
            for j in range(n_tiles):
                dl = min(cap + chunk - 1 - j, cap)
                if dl < 0:
                    bias_ref[g, j] = jnp.full((blk, blk), NEG_INF, F32)
                    continue
                bk = bucket_ref[j:j + 1, :]
                w = jnp.zeros(bk.shape, F32)
                for bucket in range(NUM_BUCKETS):
                    w = jnp.where(bk == bucket, tbl_ref[h, bucket], w)
                rows = jnp.broadcast_to(w * LOG2E, (blk, 2 * blk))
                tile = pltpu.roll(rows, blk + 1, 1, stride=1, stride_axis=0)[:, :blk]
                if dl == 0:
                    tile = jnp.where(x_idx <= y_idx, tile, NEG_INF)
                bias_ref[g, j] = tile

    @pl.when(i == 0)
    def _():
        for g in range(n_heads):
            q = q_ref[g]
            k = k_ref[g]
            kmean = jnp.sum(k.astype(F32).reshape(nb, blk, hd), axis=1) * (1.0 / blk)
            km_hi = kmean.astype(BF16)
            km_lo = (kmean - km_hi.astype(F32)).astype(BF16)
            gate = (lax.dot_general(km_hi, q, nt_dims, preferred_element_type=F32)
                    + lax.dot_general(km_lo, q, nt_dims, preferred_element_type=F32))
            n_idx = lax.broadcasted_iota(jnp.int32, gate.shape, 0)
            own = lax.shift_right_logical(
                lax.broadcasted_iota(jnp.int32, gate.shape, 1), int(math.log2(blk)))
            cnt = jnp.zeros(gate.shape, jnp.int32)
            for n in range(nb):
                row = gate[n:n + 1, :]
                beats = jnp.where(row > gate, 1, jnp.where((row == gate) & (n < n_idx), 1, 0))
                cnt = cnt + jnp.where(n < own, beats, 0)
            keep = ((n_idx < own) & (cnt < MOBA_TOPK)) | (n_idx == own)
            pen = jnp.where(keep, 0.0, NEG_INF)
            pen = jnp.concatenate([pen, jnp.zeros((hd - nb, s_len), F32)], axis=0)
            qx_ref[g, :, :hd] = q
            qx_ref[g, :, hd:] = pen.T.astype(BF16)
            row_blk = lax.shift_right_logical(
                lax.broadcasted_iota(jnp.int32, (s_len, hd), 0), int(math.log2(blk)))
            lane = lax.broadcasted_iota(jnp.int32, (s_len, hd), 1)
            kx_ref[g, :, :hd] = k
            kx_ref[g, :, hd:] = jnp.where(row_blk == lane, 1.0, 0.0).astype(BF16)
            v = v_ref[g].astype(F32)
            for c in range(s_len // cw):
                vt_ref[g, c] = v[c * cw:(c + 1) * cw, :].T.astype(BF16)

    q0 = pl.multiple_of(i * blk, blk)

    def logits(g, c, slot):
        s_ref, smax_ref = slots[slot]
        k0 = pl.multiple_of(c * cw, cw)
        s = lax.dot_general(kx_ref[g, pl.ds(k0, cw), :], qx_ref[g, pl.ds(q0, blk), :],
                            nt_dims, preferred_element_type=F32)
        j0 = jnp.maximum(cap + (chunk - 1) - (i - c * chunk), 0)
        s = s + bias_ref[g, pl.ds(j0, chunk)].reshape(cw, blk)
        s_ref[g] = s
        smax_ref[g] = jnp.max(s, axis=0, keepdims=True)

    def softmax_step(g, c, slot):
        s_ref, smax_ref = slots[slot]
        m = m_ref[g]
        m_new = jnp.maximum(m, smax_ref[g])
        alpha = jnp.exp2(m - m_new)
        p = jnp.exp2(s_ref[g] - m_new)
        m_ref[g] = m_new
        l_ref[g] = alpha * l_ref[g] + jnp.sum(p, axis=0, keepdims=True)
        acc_ref[g] = alpha * acc_ref[g] + jnp.dot(vt_ref[g, c], p.astype(BF16),
                                                   preferred_element_type=F32)

    def step(c, slot):
        for g in range(n_heads):
            logits(g, c + 1, 1 - slot)
            softmax_step(g, c, slot)

    slots = ((s0_ref, smax0_ref), (s1_ref, smax1_ref))
    last = i // chunk
    odd = last % 2
    for g in range(n_heads):
        m_ref[g] = jnp.full((1, blk), NEG_INF, F32)
        l_ref[g] = jnp.zeros((1, blk), F32)
        acc_ref[g] = jnp.zeros((hd, blk), F32)

    @pl.when(odd == 0)
    def _():
        for g in range(n_heads):
            logits(g, 0, 0)

    @pl.when(odd == 1)
    def _():
        for g in range(n_heads):
            logits(g, 0, 1)
        step(0, 1)

    def pair(j, carry):
        c = odd + 2 * j
        step(c, 0)
        step(c + 1, 1)
        return carry

    lax.fori_loop(0, last // 2, pair, 0)
    for g in range(n_heads):
        softmax_step(g, last, 0)
        o_ref[:, g * hd:(g + 1) * hd] = (acc_ref[g] * (1.0 / l_ref[g])).T.astype(BF16)


def _moba_attention(qkv, rel_bias, batch):
    _, t, hd = qkv.shape
    s = t // batch
    blk = MOBA_BLOCK
    nb = s // blk
    g, chunk = ATTN_HEADS_PER_STEP, ATTN_CHUNK_BLOCKS
    groups = N_HEADS // g
    cap = _bias_tile_cap(nb)
    n_tiles = cap + 2 * chunk - 1
    dl = jnp.minimum(cap + (chunk - 1) - jnp.arange(n_tiles, dtype=jnp.int32), cap)[:, None]
    n = jnp.arange(2 * blk, dtype=jnp.int32)[None, :]
    buckets = _t5_bucket(dl * blk + n - (blk - 1))
    once = dict(pipeline_mode=pl.Buffered(1))
    return pl.pallas_call(
        _attn_kernel,
        grid=(groups, batch, nb),
        in_specs=[
            pl.BlockSpec(memory_space=pltpu.SMEM),
            pl.BlockSpec((n_tiles, 2 * blk), lambda hg, b, i: (0, 0)),
            pl.BlockSpec((g, s, hd), lambda hg, b, i: (hg, b, 0), **once),
            pl.BlockSpec((g, s, hd), lambda hg, b, i: (groups + hg, b, 0), **once),
            pl.BlockSpec((g, s, hd), lambda hg, b, i: (2 * groups + hg, b, 0), **once),
        ],
        out_specs=pl.BlockSpec((blk, g * hd), lambda hg, b, i: (b * nb + i, hg)),
        out_shape=jax.ShapeDtypeStruct((t, N_HEADS * hd), BF16),
        scratch_shapes=[
            pltpu.VMEM((g, n_tiles, blk, blk), F32),
            pltpu.VMEM((g, s, 2 * hd), BF16),
            pltpu.VMEM((g, s, 2 * hd), BF16),
            pltpu.VMEM((g, s // (chunk * blk), hd, chunk * blk), BF16),
            pltpu.VMEM((g, chunk * blk, blk), F32),
            pltpu.VMEM((g, chunk * blk, blk), F32),
            pltpu.VMEM((g, 1, blk), F32),
            pltpu.VMEM((g, 1, blk), F32),
            pltpu.VMEM((g, 1, blk), F32),
            pltpu.VMEM((g, 1, blk), F32),
            pltpu.VMEM((g, hd, blk), F32),
        ],
        compiler_params=_params("arbitrary", "arbitrary", "arbitrary"),
        name="moba_attention",
    )(rel_bias.T.astype(F32), buckets, qkv, qkv, qkv)


def _wo_kernel(a_ref, w_ref, x_ref, o_ref):
    o_ref[...] = x_ref[...] + jnp.dot(a_ref[...], w_ref[...], preferred_element_type=F32)


def _wo_proj(a, x, w, attn_layer):
    t, d = x.shape
    tm = OUT_TM
    return pl.pallas_call(
        _wo_kernel,
        grid=(t // tm,),
        in_specs=[
            pl.BlockSpec((tm, d), lambda i: (i, 0)),
            pl.BlockSpec((None, d, d), lambda i: (attn_layer, 0, 0)),
            pl.BlockSpec((tm, d), lambda i: (i, 0)),
        ],
        out_specs=pl.BlockSpec((tm, d), lambda i: (i, 0)),
        out_shape=jax.ShapeDtypeStruct((t, d), F32),
        compiler_params=_params("parallel"),
        name="attn_wo",
    )(a, w, x)


def kernel(x, rel_bias, ffn1_norm, ffn1_w_gate, ffn1_w_up, ffn1_w_down, mix_norm, ffn2_norm, ffn2_w_gate, ffn2_w_up, ffn2_w_down, conv_pw1_w, conv_pw1_b, conv_dw_w, conv_dw_b, conv_ln_g, conv_ln_b, conv_pw2_w, conv_pw2_b, attn_wqkv, attn_q_norm, attn_k_norm, attn_wo):
    batch, seq, d = x.shape
    depth = ffn1_norm.shape[0]
    n_slabs = d // LANES
    bf = lambda w: w.astype(BF16)
    vec = lambda v: v.reshape(v.shape[0], 1, v.shape[-1])
    ffn1 = (vec(ffn1_norm), ffn1_w_gate, ffn1_w_up, ffn1_w_down)
    ffn2 = (vec(ffn2_norm), ffn2_w_gate, ffn2_w_up, ffn2_w_down)
    mix_g = vec(mix_norm)
    pw1_w, pw1_b = conv_pw1_w, vec(conv_pw1_b)
    nc = conv_dw_w.shape[0]
    dw_w = conv_dw_w.reshape(nc, CONV_WIDTH, n_slabs, LANES).transpose(0, 2, 1, 3)
    dw_b = conv_dw_b.reshape(nc, n_slabs, 1, LANES)
    pw2_w = bf(conv_pw2_w)
    wqkv, wo = attn_wqkv, bf(attn_wo)

    xt = x.reshape(batch * seq, d)
    for i in range(depth):
        xt = _ffn(xt, *ffn1, i)
        j = i // N_MIXERS
        if i % N_MIXERS == 0:
            u = _pw1_glu(xt, mix_g, pw1_w, pw1_b, i, j)
            xt = _conv_module(u, xt, dw_w, dw_b, vec(conv_ln_g), vec(conv_ln_b),
                              pw2_w, vec(conv_pw2_b), j, batch)
        else:
            qkv = _qkv_proj(xt, mix_g, wqkv, vec(attn_q_norm), vec(attn_k_norm), i, j)
            a = _moba_attention(qkv, rel_bias, batch)
            xt = _wo_proj(a, xt, wo, j)
        xt = _ffn(xt, *ffn2, i)
    return xt.reshape(batch, seq, d)
```

```python
import functools
import math

import jax
import jax.numpy as jnp
from jax import lax
from jax.experimental import pallas as pl
from jax.experimental.pallas import tpu as pltpu

D_MODEL = 2048
N_HEADS = 16
HEAD_DIM = D_MODEL // N_HEADS
D_FF = 5632
CONV_WIDTH = 31
MOBA_BLOCK = 256
MOBA_TOPK = 3
NUM_BUCKETS = 32
MAX_DISTANCE = 2048
NORM_EPS = 1e-6
NEG_INF = -1e30
N_MIXERS = 2

LANES = 128
SUBLANES = 8
VMEM_LIMIT_BYTES = 60000 * 1024

BF16 = jnp.bfloat16
F32 = jnp.float32

FFN_TM = 1024
FFN_TF = 256
PROJ_TM = 1024
PROJ_TN = 512
OUT_TM = 512
CONV_TS = 256
CONV_HALO = 32
CONV_ROWS = 64
ATTN_HEADS_PER_STEP = 4
ATTN_CHUNK_BLOCKS = 2
ATTN_SUM_ROWS = 16
LOG2E = math.log2(math.e)


def _params(*semantics):
    return pltpu.CompilerParams(
        dimension_semantics=semantics, vmem_limit_bytes=VMEM_LIMIT_BYTES)


def _rms_norm_rows(x, g):
    ms = jnp.mean(x * x, axis=-1, keepdims=True)
    return (x * lax.rsqrt(ms + NORM_EPS)) * g


def _sigmoid(x):
    return 1.0 / (1.0 + jnp.exp(-x))


def _ffn_kernel(x_hbm, g_ref, wg_ref, wu_ref, wd_ref, o_ref, x_buf, h_ref, x_sem):
    i = pl.program_id(0)
    f = pl.program_id(1)
    tm = x_buf.shape[0]

    def x_copy(tile):
        rows = pl.ds(pl.multiple_of(tile * tm, tm), tm)
        return pltpu.make_async_copy(x_hbm.at[rows, :], x_buf, x_sem)

    @pl.when((i == 0) & (f == 0))
    def _():
        x_copy(0).start()

    @pl.when(f == 0)
    def _():
        x_copy(i).wait()
        x = x_buf[...]
        h_ref[...] = _rms_norm_rows(x, g_ref[...]).astype(BF16)
        o_ref[...] = x

    @pl.when((f == 1) & (i + 1 < pl.num_programs(0)))
    def _():
        x_copy(i + 1).start()

    h = h_ref[...]
    gate = jnp.dot(h, wg_ref[...].astype(BF16), preferred_element_type=F32)
    up = jnp.dot(h, wu_ref[...].astype(BF16), preferred_element_type=F32)
    act = (gate * _sigmoid(gate)) * up * 0.5
    o_ref[...] += jnp.dot(act.astype(BF16), wd_ref[...].astype(BF16),
                          preferred_element_type=F32)


def _ffn(x, norm_g, w_gate, w_up, w_down, layer):
    t, d = x.shape
    f = w_gate.shape[-1]
    tm, tf = FFN_TM, FFN_TF
    assert f // tf >= 2
    return pl.pallas_call(
        _ffn_kernel,
        grid=(t // tm, f // tf),
        in_specs=[
            pl.BlockSpec(memory_space=pl.ANY),
            pl.BlockSpec((None, 1, d), lambda i, j: (layer, 0, 0)),
            pl.BlockSpec((None, d, tf), lambda i, j: (layer, 0, j)),
            pl.BlockSpec((None, d, tf), lambda i, j: (layer, 0, j)),
            pl.BlockSpec((None, tf, d), lambda i, j: (layer, j, 0)),
        ],
        out_specs=pl.BlockSpec((tm, d), lambda i, j: (i, 0)),
        out_shape=jax.ShapeDtypeStruct((t, d), F32),
        scratch_shapes=[
            pltpu.VMEM((tm, d), F32),
            pltpu.VMEM((tm, d), BF16),
            pltpu.SemaphoreType.DMA(()),
        ],
        compiler_params=_params("arbitrary", "arbitrary"),
        name="ffn",
    )(x, norm_g, w_gate, w_up, w_down)


def _pw1_kernel(x_ref, g_ref, wa_ref, wg_ref, ba_ref, bg_ref, o_ref, h_ref):
    @pl.when(pl.program_id(1) == 0)
    def _():
        h_ref[...] = _rms_norm_rows(x_ref[...], g_ref[...]).astype(BF16)

    h = h_ref[...]
    a = jnp.dot(h, wa_ref[...].astype(BF16), preferred_element_type=F32) + ba_ref[...]
    g = jnp.dot(h, wg_ref[...].astype(BF16), preferred_element_type=F32) + bg_ref[...]
    o_ref[...] = a * _sigmoid(g)


def _pw1_glu(x, norm_g, w, b, layer, conv_layer):
    t, d = x.shape
    tm, tn = PROJ_TM, PROJ_TN
    nj = d // tn
    return pl.pallas_call(
        _pw1_kernel,
        grid=(t // tm, nj),
        in_specs=[
            pl.BlockSpec((tm, d), lambda i, j: (i, 0)),
            pl.BlockSpec((None, 1, d), lambda i, j: (layer, 0, 0)),
            pl.BlockSpec((None, d, tn), lambda i, j: (conv_layer, 0, j)),
            pl.BlockSpec((None, d, tn), lambda i, j: (conv_layer, 0, j + nj)),
            pl.BlockSpec((None, 1, tn), lambda i, j: (conv_layer, 0, j)),
            pl.BlockSpec((None, 1, tn), lambda i, j: (conv_layer, 0, j + nj)),
        ],
        out_specs=pl.BlockSpec((tm, tn), lambda i, j: (i, j)),
        out_shape=jax.ShapeDtypeStruct((t, d), F32),
        scratch_shapes=[pltpu.VMEM((tm, d), BF16)],
        compiler_params=_params("parallel", "arbitrary"),
        name="conv_pw1_glu",
    )(x, norm_g, w, w, b, b)


def _conv_kernel(u_ref, halo_ref, dww_ref, dwb_ref, lng_ref, lnb_ref, w2_ref,
                 b2_ref, x_ref, o_ref, ext_ref, cv_ref, y_ref):
    ts, d = u_ref.shape
    n_slabs = d // LANES
    pad = CONV_HALO - (CONV_WIDTH - 1)
    first = pl.program_id(1) == 0

    for c in range(n_slabs):
        lanes = slice(c * LANES, (c + 1) * LANES)
        halo = halo_ref[:, lanes]
        ext_ref[c, 0:CONV_HALO, :] = jnp.where(first, 0.0, halo)
        ext_ref[c, CONV_HALO:, :] = u_ref[:, lanes]

    def slab_body(c, carry):
        w_rows = [dww_ref[c, k:k + 1, :] for k in range(CONV_WIDTH)]
        bias = dwb_ref[c]

        def row_body(r, carry2):
            r0 = pl.multiple_of(r * CONV_ROWS, CONV_ROWS)
            acc = jnp.broadcast_to(bias, (CONV_ROWS, LANES))
            for k in range(CONV_WIDTH):
                acc = acc + w_rows[k] * ext_ref[c, pl.ds(r0 + pad + k, CONV_ROWS), :]
            cv_ref[c, pl.ds(r0, CONV_ROWS), :] = acc
            return carry2

        return lax.fori_loop(0, ts // CONV_ROWS, row_body, carry)

    lax.fori_loop(0, n_slabs, slab_body, 0)

    total = cv_ref[0]
    for c in range(1, n_slabs):
        total = total + cv_ref[c]
    mu = jnp.sum(total, axis=-1, keepdims=True) * (1.0 / d)
    sq = jnp.zeros((ts, LANES), F32)
    for c in range(n_slabs):
        xc = cv_ref[c] - mu
        sq = sq + xc * xc
    rstd = lax.rsqrt(jnp.sum(sq, axis=-1, keepdims=True) * (1.0 / d) + NORM_EPS)
    for c in range(n_slabs):
        lanes = slice(c * LANES, (c + 1) * LANES)
        y = (cv_ref[c] - mu) * rstd * lng_ref[:, lanes] + lnb_ref[:, lanes]
        y_ref[:, lanes] = (y * _sigmoid(y)).astype(BF16)

    o_ref[...] = (x_ref[...] + b2_ref[...]
                  + jnp.dot(y_ref[...], w2_ref[...], preferred_element_type=F32))


def _conv_module(u, x, dw_w, dw_b, ln_g, ln_b, w2, b2, conv_layer, batch):
    t, d = x.shape
    s = t // batch
    ts = CONV_TS
    n_slabs = d // LANES
    spb = s // ts
    hpt = ts // CONV_HALO

    def halo_map(b, i):
        return (jnp.maximum((b * spb + i) * hpt - 1, 0), 0)

    row = lambda b, i: (b * spb + i, 0)
    vec = lambda b, i: (conv_layer, 0, 0)
    return pl.pallas_call(
        _conv_kernel,
        grid=(batch, spb),
        in_specs=[
            pl.BlockSpec((ts, d), row),
            pl.BlockSpec((CONV_HALO, d), halo_map),
            pl.BlockSpec((None, n_slabs, CONV_WIDTH, LANES), lambda b, i: (conv_layer, 0, 0, 0)),
            pl.BlockSpec((None, n_slabs, 1, LANES), lambda b, i: (conv_layer, 0, 0, 0)),
            pl.BlockSpec((None, 1, d), vec),
            pl.BlockSpec((None, 1, d), vec),
            pl.BlockSpec((None, d, d), vec),
            pl.BlockSpec((None, 1, d), vec),
            pl.BlockSpec((ts, d), row),
        ],
        out_specs=pl.BlockSpec((ts, d), row),
        out_shape=jax.ShapeDtypeStruct((t, d), F32),
        scratch_shapes=[
            pltpu.VMEM((n_slabs, ts + CONV_HALO, LANES), F32),
            pltpu.VMEM((n_slabs, ts, LANES), F32),
            pltpu.VMEM((ts, d), BF16),
        ],
        compiler_params=_params("parallel", "arbitrary"),
        name="conv_dw_ln_pw2",
    )(u, u, dw_w, dw_b, ln_g, ln_b, w2, b2, x)


def _qkv_kernel(x_ref, g_ref, w_ref, qn_ref, kn_ref, o_ref, h_ref):
    j = pl.program_id(1)
    heads_per_step = o_ref.shape[0]
    n_q_steps = N_HEADS // heads_per_step

    @pl.when(j == 0)
    def _():
        h_ref[...] = _rms_norm_rows(x_ref[...], g_ref[...]).astype(BF16)

    y = jnp.dot(h_ref[...], w_ref[...].astype(BF16), preferred_element_type=F32)
    is_q = j < n_q_steps
    is_v = j >= 2 * n_q_steps
    gain = jnp.where(is_q, qn_ref[...] * (HEAD_DIM ** -0.5 * LOG2E), kn_ref[...])
    for c in range(heads_per_step):
        t = y[:, c * HEAD_DIM:(c + 1) * HEAD_DIM]
        normed = _rms_norm_rows(t, gain)
        o_ref[c] = jnp.where(is_v, t, normed).astype(BF16)


def _qkv_proj(x, norm_g, w, q_norm, k_norm, layer, attn_layer):
    t, d = x.shape
    tm, tn = PROJ_TM, PROJ_TN
    hps = tn // HEAD_DIM
    return pl.pallas_call(
        _qkv_kernel,
        grid=(t // tm, 3 * d // tn),
        in_specs=[
            pl.BlockSpec((tm, d), lambda i, j: (i, 0)),
            pl.BlockSpec((None, 1, d), lambda i, j: (layer, 0, 0)),
            pl.BlockSpec((None, d, tn), lambda i, j: (attn_layer, 0, j)),
            pl.BlockSpec((None, 1, HEAD_DIM), lambda i, j: (attn_layer, 0, 0)),
            pl.BlockSpec((None, 1, HEAD_DIM), lambda i, j: (attn_layer, 0, 0)),
        ],
        out_specs=pl.BlockSpec((hps, tm, HEAD_DIM), lambda i, j: (j, i, 0)),
        out_shape=jax.ShapeDtypeStruct((3 * N_HEADS, t, HEAD_DIM), BF16),
        scratch_shapes=[pltpu.VMEM((tm, d), BF16)],
        compiler_params=_params("parallel", "arbitrary"),
        name="attn_qkv",
    )(x, norm_g, w, q_norm, k_norm)


def _t5_bucket(dist):
    max_exact = NUM_BUCKETS // 2
    n = jnp.maximum(dist, 0)
    nf = jnp.maximum(n, 1).astype(F32)
    large = max_exact + (jnp.log(nf / max_exact) / math.log(MAX_DISTANCE / max_exact)
                         * (NUM_BUCKETS - max_exact)).astype(jnp.int32)
    large = jnp.minimum(large, NUM_BUCKETS - 1)
    return jnp.where(n < max_exact, n, large)


def _bias_tile_cap(n_blocks):
    max_exact = NUM_BUCKETS // 2
    d = max_exact
    while max_exact + int(math.log(d / max_exact) / math.log(MAX_DISTANCE / max_exact)
                          * (NUM_BUCKETS - max_exact)) < NUM_BUCKETS - 1:
        d += 1
    cap = -(-(d + MOBA_BLOCK - 1) // MOBA_BLOCK)
    return min(cap, n_blocks - 1)


def _attn_kernel(tbl_ref, bucket_ref, q_ref, k_ref, v_ref, o_ref,
                 bias_ref, pen_ref, vt_ref, s0_ref, s1_ref, smax0_ref, smax1_ref,
                 m_ref, acc_ref):
    hg = pl.program_id(0)
    b = pl.program_id(1)
    i = pl.program_id(2)
    blk = MOBA_BLOCK
    n_heads, s_len, hd = q_ref.shape
    nb = s_len // blk
    chunk = ATTN_CHUNK_BLOCKS
    cw = chunk * blk
    n_tiles = bias_ref.shape[1]
    cap = n_tiles - (2 * chunk - 1)
    nt_dims = (((1,), (1,)), ((), ()))

    @pl.when((b == 0) & (i == 0))
    def _():
        x_idx = lax.broadcasted_iota(jnp.int32, (blk, blk), 0)
        y_idx = lax.broadcasted_iota(jnp.int32, (blk, blk), 1)
        for g in range(n_heads):
            h = hg * n_heads + g
            for j in range(n_tiles):
                dl = min(cap + chunk - 1 - j, cap)
                if dl < 0:
                    bias_ref[g, j] = jnp.full((blk, blk), NEG_INF, F32)
                    continue
                bk = bucket_ref[j:j + 1, :]
                w = jnp.zeros(bk.shape, F32)
                for bucket in range(NUM_BUCKETS):
                    w = jnp.where(bk == bucket, tbl_ref[h, bucket], w)
                rows = jnp.broadcast_to(w * LOG2E, (blk, 2 * blk))
                tile = pltpu.roll(rows, blk + 1, 1, stride=1, stride_axis=0)[:, :blk]
                if dl == 0:
                    tile = jnp.where(x_idx <= y_idx, tile, NEG_INF)
                bias_ref[g, j] = tile

    @pl.when(i == 0)
    def _():
        for g in range(n_heads):
            q = q_ref[g]
            k = k_ref[g]
            kmean = jnp.sum(k.astype(F32).reshape(nb, blk, hd), axis=1) * (1.0 / blk)
            km_hi = kmean.astype(BF16)
            km_lo = (kmean - km_hi.astype(F32)).astype(BF16)
            gate = (lax.dot_general(km_hi, q, nt_dims, preferred_element_type=F32)
                    + lax.dot_general(km_lo, q, nt_dims, preferred_element_type=F32))
            n_idx = lax.broadcasted_iota(jnp.int32, gate.shape, 0)
            own = lax.shift_right_logical(
                lax.broadcasted_iota(jnp.int32, gate.shape, 1), int(math.log2(blk)))
            cnt = jnp.zeros(gate.shape, jnp.int32)
            for n in range(nb):
                row = gate[n:n + 1, :]
                beats = jnp.where(row > gate, 1, jnp.where((row == gate) & (n < n_idx), 1, 0))
                cnt = cnt + jnp.where(n < own, beats, 0)
            keep = ((n_idx < own) & (cnt < MOBA_TOPK)) | (n_idx == own)
            pen = jnp.where(keep, 0.0, NEG_INF)
            for t in range(s_len // cw):
                pen_ref[g, t] = pen[:, t * cw:(t + 1) * cw]
            v = v_ref[g].astype(F32)
            ones_row = (lax.broadcasted_iota(jnp.int32, (ATTN_SUM_ROWS, cw), 0) == 0)
            for c in range(s_len // cw):
                vt_ref[g, c, :hd, :] = v[c * cw:(c + 1) * cw, :].T.astype(BF16)
                vt_ref[g, c, hd:, :] = jnp.where(ones_row, 1.0, 0.0).astype(BF16)

    q0 = pl.multiple_of(i * cw, cw)

    def pen_rows(g, c):
        return [pen_ref[g, i, pl.ds(c * chunk + kb, 1), :] for kb in range(chunk)]

    def logits(g, c, slot):
        s_ref, smax_ref = slots[slot]
        k0 = pl.multiple_of(c * cw, cw)
        s = lax.dot_general(k_ref[g, pl.ds(k0, cw), :], q_ref[g, pl.ds(q0, cw), :],
                            nt_dims, preferred_element_type=F32)
        pen = pen_rows(g, c)
        for qb in range(chunk):
            cols = slice(qb * blk, (qb + 1) * blk)
            j0 = jnp.maximum(cap + (chunk - 1) - ((i - c) * chunk + qb), 0)
            sq = s[:, cols] + bias_ref[g, pl.ds(j0, chunk)].reshape(cw, blk)
            s_ref[g, :, cols] = sq
            block_max = [jnp.max(sq[kb * blk:(kb + 1) * blk, :], axis=0, keepdims=True)
                         + pen[kb][:, cols] for kb in range(chunk)]
            smax_ref[g, :, cols] = functools.reduce(jnp.maximum, block_max)

    def softmax_step(g, c, slot):
        s_ref, smax_ref = slots[slot]
        m = m_ref[g]
        m_new = jnp.maximum(m, smax_ref[g])
        alpha = jnp.exp2(m - m_new)
        p = jnp.concatenate(
            [jnp.exp2(s_ref[g, kb * blk:(kb + 1) * blk, :] - jnp.where(pen_kb < 0.0, -NEG_INF, m_new))
             for kb, pen_kb in enumerate(pen_rows(g, c))], axis=0)
        m_ref[g] = m_new
        acc_ref[g] = alpha * acc_ref[g] + jnp.dot(vt_ref[g, c], p.astype(BF16),
                                                   preferred_element_type=F32)

    def step(c, slot):
        for g in range(n_heads):
            logits(g, c + 1, 1 - slot)
            softmax_step(g, c, slot)

    slots = ((s0_ref, smax0_ref), (s1_ref, smax1_ref))
    odd = i % 2
    for g in range(n_heads):
        m_ref[g] = jnp.full((1, cw), NEG_INF, F32)
        acc_ref[g] = jnp.zeros(acc_ref.shape[1:], F32)

    @pl.when(odd == 0)
    def _():
        for g in range(n_heads):
            logits(g, 0, 0)

    @pl.when(odd == 1)
    def _():
        for g in range(n_heads):
            logits(g, 0, 1)
        step(0, 1)

    def pair(j, carry):
        c = odd + 2 * j
        step(c, 0)
        step(c + 1, 1)
        return carry

    lax.fori_loop(0, i // 2, pair, 0)
    for g in range(n_heads):
        softmax_step(g, i, 0)
        acc = acc_ref[g]
        out = acc[:hd, :] * (1.0 / acc[hd:hd + 1, :])
        o_ref[:, g * hd:(g + 1) * hd] = out.T.astype(BF16)


def _moba_attention(qkv, rel_bias, batch):
    _, t, hd = qkv.shape
    s = t // batch
    blk = MOBA_BLOCK
    nb = s // blk
    g, chunk = ATTN_HEADS_PER_STEP, ATTN_CHUNK_BLOCKS
    groups = N_HEADS // g
    cap = _bias_tile_cap(nb)
    n_tiles = cap + 2 * chunk - 1
    dl = jnp.minimum(cap + (chunk - 1) - jnp.arange(n_tiles, dtype=jnp.int32), cap)[:, None]
    n = jnp.arange(2 * blk, dtype=jnp.int32)[None, :]
    buckets = _t5_bucket(dl * blk + n - (blk - 1))
    once = dict(pipeline_mode=pl.Buffered(1))
    return pl.pallas_call(
        _attn_kernel,
        grid=(groups, batch, nb // chunk),
        in_specs=[
            pl.BlockSpec(memory_space=pltpu.SMEM),
            pl.BlockSpec((n_tiles, 2 * blk), lambda hg, b, i: (0, 0)),
            pl.BlockSpec((g, s, hd), lambda hg, b, i: (hg, b, 0), **once),
            pl.BlockSpec((g, s, hd), lambda hg, b, i: (groups + hg, b, 0), **once),
            pl.BlockSpec((g, s, hd), lambda hg, b, i: (2 * groups + hg, b, 0), **once),
        ],
        out_specs=pl.BlockSpec((chunk * blk, g * hd), lambda hg, b, i: (b * (nb // chunk) + i, hg)),
        out_shape=jax.ShapeDtypeStruct((t, N_HEADS * hd), BF16),
        scratch_shapes=[
            pltpu.VMEM((g, n_tiles, blk, blk), F32),
            pltpu.VMEM((g, s // (chunk * blk), nb, chunk * blk), F32),
            pltpu.VMEM((g, s // (chunk * blk), hd + ATTN_SUM_ROWS, chunk * blk), BF16),
            pltpu.VMEM((g, chunk * blk, chunk * blk), F32),
            pltpu.VMEM((g, chunk * blk, chunk * blk), F32),
            pltpu.VMEM((g, 1, chunk * blk), F32),
            pltpu.VMEM((g, 1, chunk * blk), F32),
            pltpu.VMEM((g, 1, chunk * blk), F32),
            pltpu.VMEM((g, hd + ATTN_SUM_ROWS, chunk * blk), F32),
        ],
        compiler_params=_params("arbitrary", "arbitrary", "arbitrary"),
        name="moba_attention",
    )(rel_bias.T.astype(F32), buckets, qkv, qkv, qkv)


def _wo_kernel(a_ref, w_ref, x_ref, o_ref):
    o_ref[...] = x_ref[...] + jnp.dot(a_ref[...], w_ref[...], preferred_element_type=F32)


def _wo_proj(a, x, w, attn_layer):
    t, d = x.shape
    tm = OUT_TM
    return pl.pallas_call(
        _wo_kernel,
        grid=(t // tm,),
        in_specs=[
            pl.BlockSpec((tm, d), lambda i: (i, 0)),
            pl.BlockSpec((None, d, d), lambda i: (attn_layer, 0, 0)),
            pl.BlockSpec((tm, d), lambda i: (i, 0)),
        ],
        out_specs=pl.BlockSpec((tm, d), lambda i: (i, 0)),
        out_shape=jax.ShapeDtypeStruct((t, d), F32),
        compiler_params=_params("parallel"),
        name="attn_wo",
    )(a, w, x)


def kernel(x, rel_bias, ffn1_norm, ffn1_w_gate, ffn1_w_up, ffn1_w_down, mix_norm, ffn2_norm, ffn2_w_gate, ffn2_w_up, ffn2_w_down, conv_pw1_w, conv_pw1_b, conv_dw_w, conv_dw_b, conv_ln_g, conv_ln_b, conv_pw2_w, conv_pw2_b, attn_wqkv, attn_q_norm, attn_k_norm, attn_wo):
    batch, seq, d = x.shape
    depth = ffn1_norm.shape[0]
    n_slabs = d // LANES
    bf = lambda w: w.astype(BF16)
    vec = lambda v: v.reshape(v.shape[0], 1, v.shape[-1])
    ffn1 = (vec(ffn1_norm), ffn1_w_gate, ffn1_w_up, ffn1_w_down)
    ffn2 = (vec(ffn2_norm), ffn2_w_gate, ffn2_w_up, ffn2_w_down)
    mix_g = vec(mix_norm)
    pw1_w, pw1_b = conv_pw1_w, vec(conv_pw1_b)
    nc = conv_dw_w.shape[0]
    dw_w = conv_dw_w.reshape(nc, CONV_WIDTH, n_slabs, LANES).transpose(0, 2, 1, 3)
    dw_b = conv_dw_b.reshape(nc, n_slabs, 1, LANES)
    pw2_w = bf(conv_pw2_w)
    wqkv, wo = attn_wqkv, bf(attn_wo)

    xt = x.reshape(batch * seq, d)
    for i in range(depth):
        xt = _ffn(xt, *ffn1, i)
        j = i // N_MIXERS
        if i % N_MIXERS == 0:
            u = _pw1_glu(xt, mix_g, pw1_w, pw1_b, i, j)
            xt = _conv_module(u, xt, dw_w, dw_b, vec(conv_ln_g), vec(conv_ln_b),
                              pw2_w, vec(conv_pw2_b), j, batch)
        else:
            qkv = _qkv_proj(xt, mix_g, wqkv, vec(attn_q_norm), vec(attn_k_norm), i, j)
            a = _moba_attention(qkv, rel_bias, batch)
            xt = _wo_proj(a, xt, wo, j)
        xt = _ffn(xt, *ffn2, i)
    return xt.reshape(batch, seq, d)
```

```python
import functools
import math

import jax
import jax.numpy as jnp
from jax import lax
from jax.experimental import pallas as pl
from jax.experimental.pallas import tpu as pltpu

D_MODEL = 2048
N_HEADS = 16
HEAD_DIM = D_MODEL // N_HEADS
D_FF = 5632
CONV_WIDTH = 31
MOBA_BLOCK = 256
MOBA_TOPK = 3
NUM_BUCKETS = 32
MAX_DISTANCE = 2048
NORM_EPS = 1e-6
NEG_INF = -1e30
N_MIXERS = 2

LANES = 128
SUBLANES = 8
VMEM_LIMIT_BYTES = 60000 * 1024

BF16 = jnp.bfloat16
F32 = jnp.float32

FFN_TM = 1024
FFN_TF = 256
PROJ_TM = 1024
PROJ_TN = 512
QKV_TN = 1024
OUT_TM = 512
CONV_TS = 256
CONV_HALO = 32
CONV_ROWS = 64
CONV_PW2_COLS = 256
ATTN_HEADS_PER_STEP = 4
ATTN_CHUNK_BLOCKS = 2
ATTN_SUM_ROWS = 16
LOG2E = math.log2(math.e)


def _params(*semantics):
    return pltpu.CompilerParams(
        dimension_semantics=semantics, vmem_limit_bytes=VMEM_LIMIT_BYTES)


def _rms_norm_rows(x, g):
    ms = jnp.mean(x * x, axis=-1, keepdims=True)
    return (x * lax.rsqrt(ms + NORM_EPS)) * g


def _sigmoid(x):
    return 1.0 / (1.0 + jnp.exp(-x))


def _ffn_kernel(x_hbm, g_ref, wg_ref, wu_ref, wd_ref, o_ref, x_buf, h_ref, x_sem):
    i = pl.program_id(0)
    f = pl.program_id(1)
    tm = x_buf.shape[0]

    def x_copy(tile):
        rows = pl.ds(pl.multiple_of(tile * tm, tm), tm)
        return pltpu.make_async_copy(x_hbm.at[rows, :], x_buf, x_sem)

    @pl.when((i == 0) & (f == 0))
    def _():
        x_copy(0).start()

    @pl.when(f == 0)
    def _():
        x_copy(i).wait()
        x = x_buf[...]
        h_ref[...] = _rms_norm_rows(x, g_ref[...]).astype(BF16)
        o_ref[...] = x

    @pl.when((f == 1) & (i + 1 < pl.num_programs(0)))
    def _():
        x_copy(i + 1).start()

    h = h_ref[...]
    gate = jnp.dot(h, wg_ref[...].astype(BF16), preferred_element_type=F32)
    up = jnp.dot(h, wu_ref[...].astype(BF16), preferred_element_type=F32)
    act = (gate * _sigmoid(gate)) * up * 0.5
    o_ref[...] += jnp.dot(act.astype(BF16), wd_ref[...].astype(BF16),
                          preferred_element_type=F32)


def _ffn(x, norm_g, w_gate, w_up, w_down, layer):
    t, d = x.shape
    f = w_gate.shape[-1]
    tm, tf = FFN_TM, FFN_TF
    assert f // tf >= 2
    return pl.pallas_call(
        _ffn_kernel,
        grid=(t // tm, f // tf),
        in_specs=[
            pl.BlockSpec(memory_space=pl.ANY),
            pl.BlockSpec((None, 1, d), lambda i, j: (layer, 0, 0)),
            pl.BlockSpec((None, d, tf), lambda i, j: (layer, 0, j)),
            pl.BlockSpec((None, d, tf), lambda i, j: (layer, 0, j)),
            pl.BlockSpec((None, tf, d), lambda i, j: (layer, j, 0)),
        ],
        out_specs=pl.BlockSpec((tm, d), lambda i, j: (i, 0)),
        out_shape=jax.ShapeDtypeStruct((t, d), F32),
        scratch_shapes=[
            pltpu.VMEM((tm, d), F32),
            pltpu.VMEM((tm, d), BF16),
            pltpu.SemaphoreType.DMA(()),
        ],
        compiler_params=_params("arbitrary", "arbitrary"),
        name="ffn",
    )(x, norm_g, w_gate, w_up, w_down)


def _pw1_kernel(x_ref, g_ref, wa_ref, wg_ref, ba_ref, bg_ref, o_ref, h_ref):
    @pl.when(pl.program_id(1) == 0)
    def _():
        h_ref[...] = _rms_norm_rows(x_ref[...], g_ref[...]).astype(BF16)

    h = h_ref[...]
    a = jnp.dot(h, wa_ref[...].astype(BF16), preferred_element_type=F32) + ba_ref[...]
    g = jnp.dot(h, wg_ref[...].astype(BF16), preferred_element_type=F32) + bg_ref[...]
    o_ref[...] = a * _sigmoid(g)


def _pw1_glu(x, norm_g, w, b, layer, conv_layer):
    t, d = x.shape
    tm, tn = PROJ_TM, PROJ_TN
    nj = d // tn
    return pl.pallas_call(
        _pw1_kernel,
        grid=(t // tm, nj),
        in_specs=[
            pl.BlockSpec((tm, d), lambda i, j: (i, 0)),
            pl.BlockSpec((None, 1, d), lambda i, j: (layer, 0, 0)),
            pl.BlockSpec((None, d, tn), lambda i, j: (conv_layer, 0, j)),
            pl.BlockSpec((None, d, tn), lambda i, j: (conv_layer, 0, j + nj)),
            pl.BlockSpec((None, 1, tn), lambda i, j: (conv_layer, 0, j)),
            pl.BlockSpec((None, 1, tn), lambda i, j: (conv_layer, 0, j + nj)),
        ],
        out_specs=pl.BlockSpec((tm, tn), lambda i, j: (i, j)),
        out_shape=jax.ShapeDtypeStruct((t, d), F32),
        scratch_shapes=[pltpu.VMEM((tm, d), BF16)],
        compiler_params=_params("parallel", "arbitrary"),
        name="conv_pw1_glu",
    )(x, norm_g, w, w, b, b)


def _conv_kernel(tiles_per_seq, u_ref, halo_ref, dww_ref, dwb_ref, lng_ref, lnb_ref, w2_ref,
                 b2_ref, x_ref, o_ref, ext_ref, cv_ref, y_ref):
    ts, d = u_ref.shape
    n_slabs = d // LANES
    pad = CONV_HALO - (CONV_WIDTH - 1)
    s = pl.program_id(0)
    conv_tile = jnp.minimum(s, pl.num_programs(0) - 2)
    first = lax.rem(conv_tile, tiles_per_seq) == 0

    @pl.when(s == 0)
    def _():
        y_ref[...] = jnp.zeros(y_ref.shape, BF16)

    for c in range(n_slabs):
        lanes = slice(c * LANES, (c + 1) * LANES)
        halo = halo_ref[:, lanes]
        ext_ref[c, 0:CONV_HALO, :] = jnp.where(first, 0.0, halo)
        ext_ref[c, CONV_HALO:, :] = u_ref[:, lanes]

    slabs_per_piece = CONV_PW2_COLS // LANES
    for piece in range(d // CONV_PW2_COLS):
        for c in range(piece * slabs_per_piece, (piece + 1) * slabs_per_piece):
            w_rows = [dww_ref[c, k:k + 1, :] for k in range(CONV_WIDTH)]
            bias = dwb_ref[c]
            for r0 in range(0, ts, CONV_ROWS):
                acc = jnp.broadcast_to(bias, (CONV_ROWS, LANES))
                for k in range(CONV_WIDTH):
                    acc = acc + w_rows[k] * ext_ref[c, r0 + pad + k:r0 + pad + k + CONV_ROWS, :]
                cv_ref[c, r0:r0 + CONV_ROWS, :] = acc
        cols = slice(piece * CONV_PW2_COLS, (piece + 1) * CONV_PW2_COLS)
        o_ref[:, cols] = (x_ref[:, cols] + b2_ref[:, cols]
                          + jnp.dot(y_ref[...], w2_ref[:, cols], preferred_element_type=F32))

    total = cv_ref[0]
    for c in range(1, n_slabs):
        total = total + cv_ref[c]
    mu = jnp.sum(total, axis=-1, keepdims=True) * (1.0 / d)
    sq = jnp.zeros((ts, LANES), F32)
    for c in range(n_slabs):
        xc = cv_ref[c] - mu
        sq = sq + xc * xc
    rstd = lax.rsqrt(jnp.sum(sq, axis=-1, keepdims=True) * (1.0 / d) + NORM_EPS)
    for c in range(n_slabs):
        lanes = slice(c * LANES, (c + 1) * LANES)
        y = (cv_ref[c] - mu) * rstd * lng_ref[:, lanes] + lnb_ref[:, lanes]
        y_ref[:, lanes] = (y * _sigmoid(y)).astype(BF16)


def _conv_module(u, x, dw_w, dw_b, ln_g, ln_b, w2, b2, conv_layer, batch):
    t, d = x.shape
    ts = CONV_TS
    n_slabs = d // LANES
    n_tiles = t // ts
    tiles_per_seq = t // batch // ts
    hpt = ts // CONV_HALO

    conv_row = lambda s: (jnp.minimum(s, n_tiles - 1), 0)
    halo_row = lambda s: (jnp.maximum(jnp.minimum(s, n_tiles - 1) * hpt - 1, 0), 0)
    pw2_row = lambda s: (jnp.maximum(s - 1, 0), 0)
    vec = lambda s: (conv_layer, 0, 0)
    return pl.pallas_call(
        functools.partial(_conv_kernel, tiles_per_seq),
        grid=(n_tiles + 1,),
        in_specs=[
            pl.BlockSpec((ts, d), conv_row),
            pl.BlockSpec((CONV_HALO, d), halo_row),
            pl.BlockSpec((None, n_slabs, CONV_WIDTH, LANES), lambda s: (conv_layer, 0, 0, 0)),
            pl.BlockSpec((None, n_slabs, 1, LANES), lambda s: (conv_layer, 0, 0, 0)),
            pl.BlockSpec((None, 1, d), vec),
            pl.BlockSpec((None, 1, d), vec),
            pl.BlockSpec((None, d, d), vec),
            pl.BlockSpec((None, 1, d), vec),
            pl.BlockSpec((ts, d), pw2_row),
        ],
        out_specs=pl.BlockSpec((ts, d), pw2_row),
        out_shape=jax.ShapeDtypeStruct((t, d), F32),
        scratch_shapes=[
            pltpu.VMEM((n_slabs, ts + CONV_HALO, LANES), F32),
            pltpu.VMEM((n_slabs, ts, LANES), F32),
            pltpu.VMEM((ts, d), BF16),
        ],
        compiler_params=_params("arbitrary"),
        name="conv_dw_ln_pw2",
    )(u, u, dw_w, dw_b, ln_g, ln_b, w2, b2, x)


def _qkv_kernel(x_ref, g_ref, w_ref, qn_ref, kn_ref, o_ref, h_ref):
    j = pl.program_id(1)
    heads_per_step = o_ref.shape[0]
    n_q_steps = N_HEADS // heads_per_step

    @pl.when(j == 0)
    def _():
        h_ref[...] = _rms_norm_rows(x_ref[...], g_ref[...]).astype(BF16)

    y = jnp.dot(h_ref[...], w_ref[...].astype(BF16), preferred_element_type=F32)
    is_q = j < n_q_steps
    is_v = j >= 2 * n_q_steps
    gain = jnp.where(is_q, qn_ref[...] * (HEAD_DIM ** -0.5 * LOG2E), kn_ref[...])
    for c in range(heads_per_step):
        t = y[:, c * HEAD_DIM:(c + 1) * HEAD_DIM]
        normed = _rms_norm_rows(t, gain)
        o_ref[c] = jnp.where(is_v, t, normed).astype(BF16)


def _qkv_proj(x, norm_g, w, q_norm, k_norm, layer, attn_layer):
    t, d = x.shape
    tm, tn = PROJ_TM, QKV_TN
    hps = tn // HEAD_DIM
    return pl.pallas_call(
        _qkv_kernel,
        grid=(t // tm, 3 * d // tn),
        in_specs=[
            pl.BlockSpec((tm, d), lambda i, j: (i, 0)),
            pl.BlockSpec((None, 1, d), lambda i, j: (layer, 0, 0)),
            pl.BlockSpec((None, d, tn), lambda i, j: (attn_layer, 0, j)),
            pl.BlockSpec((None, 1, HEAD_DIM), lambda i, j: (attn_layer, 0, 0)),
            pl.BlockSpec((None, 1, HEAD_DIM), lambda i, j: (attn_layer, 0, 0)),
        ],
        out_specs=pl.BlockSpec((hps, tm, HEAD_DIM), lambda i, j: (j, i, 0)),
        out_shape=jax.ShapeDtypeStruct((3 * N_HEADS, t, HEAD_DIM), BF16),
        scratch_shapes=[pltpu.VMEM((tm, d), BF16)],
        compiler_params=_params("parallel", "arbitrary"),
        name="attn_qkv",
    )(x, norm_g, w, q_norm, k_norm)


def _t5_bucket(dist):
    max_exact = NUM_BUCKETS // 2
    n = jnp.maximum(dist, 0)
    nf = jnp.maximum(n, 1).astype(F32)
    large = max_exact + (jnp.log(nf / max_exact) / math.log(MAX_DISTANCE / max_exact)
                         * (NUM_BUCKETS - max_exact)).astype(jnp.int32)
    large = jnp.minimum(large, NUM_BUCKETS - 1)
    return jnp.where(n < max_exact, n, large)


def _bias_tile_cap(n_blocks):
    max_exact = NUM_BUCKETS // 2
    d = max_exact
    while max_exact + int(math.log(d / max_exact) / math.log(MAX_DISTANCE / max_exact)
                          * (NUM_BUCKETS - max_exact)) < NUM_BUCKETS - 1:
        d += 1
    cap = -(-(d + MOBA_BLOCK - 1) // MOBA_BLOCK)
    return min(cap, n_blocks - 1)


def _attn_kernel(tbl_ref, bucket_ref, q_ref, k_ref, v_ref, o_ref,
                 bias_ref, pen_ref, vt_ref, s0_ref, s1_ref, smax0_ref, smax1_ref,
                 m_ref, acc_ref):
    hg = pl.program_id(0)
    b = pl.program_id(1)
    i = pl.program_id(2)
    blk = MOBA_BLOCK
    n_heads, s_len, hd = q_ref.shape
    nb = s_len // blk
    chunk = ATTN_CHUNK_BLOCKS
    cw = chunk * blk
    n_tiles = bias_ref.shape[1]
    cap = n_tiles - (2 * chunk - 1)
    nt_dims = (((1,), (1,)), ((), ()))

    @pl.when((b == 0) & (i == 0))
    def _():
        x_idx = lax.broadcasted_iota(jnp.int32, (blk, blk), 0)
        y_idx = lax.broadcasted_iota(jnp.int32, (blk, blk), 1)
        for g in range(n_heads):
            h = hg * n_heads + g
            for j in range(n_tiles):
                dl = min(cap + chunk - 1 - j, cap)
                if dl < 0:
                    bias_ref[g, j] = jnp.full((blk, blk), NEG_INF, F32)
                    continue
                bk = bucket_ref[j:j + 1, :]
                w = jnp.zeros(bk.shape, F32)
                for bucket in range(NUM_BUCKETS):
                    w = jnp.where(bk == bucket, tbl_ref[h, bucket], w)
                rows = jnp.broadcast_to(w * LOG2E, (blk, 2 * blk))
                tile = pltpu.roll(rows, blk + 1, 1, stride=1, stride_axis=0)[:, :blk]
                if dl == 0:
                    tile = jnp.where(x_idx <= y_idx, tile, NEG_INF)
                bias_ref[g, j] = tile

    @pl.when(i == 0)
    def _():
        for g in range(n_heads):
            q = q_ref[g]
            k = k_ref[g]
            kmean = jnp.sum(k.astype(F32).reshape(nb, blk, hd), axis=1) * (1.0 / blk)
            km_hi = kmean.astype(BF16)
            km_lo = (kmean - km_hi.astype(F32)).astype(BF16)
            gate = (lax.dot_general(km_hi, q, nt_dims, preferred_element_type=F32)
                    + lax.dot_general(km_lo, q, nt_dims, preferred_element_type=F32))
            n_idx = lax.broadcasted_iota(jnp.int32, gate.shape, 0)
            own = lax.shift_right_logical(
                lax.broadcasted_iota(jnp.int32, gate.shape, 1), int(math.log2(blk)))
            cnt = jnp.zeros(gate.shape, jnp.int32)
            for n in range(nb):
                row = gate[n:n + 1, :]
                beats = jnp.where(row > gate, 1, jnp.where((row == gate) & (n < n_idx), 1, 0))
                cnt = cnt + jnp.where(n < own, beats, 0)
            keep = ((n_idx < own) & (cnt < MOBA_TOPK)) | (n_idx == own)
            pen = jnp.where(keep, 0.0, NEG_INF)
            for t in range(s_len // cw):
                pen_ref[g, t] = pen[:, t * cw:(t + 1) * cw]
            v = v_ref[g].astype(F32)
            ones_row = (lax.broadcasted_iota(jnp.int32, (ATTN_SUM_ROWS, cw), 0) == 0)
            for c in range(s_len // cw):
                vt_ref[g, c, :hd, :] = v[c * cw:(c + 1) * cw, :].T.astype(BF16)
                vt_ref[g, c, hd:, :] = jnp.where(ones_row, 1.0, 0.0).astype(BF16)

    q0 = pl.multiple_of(i * cw, cw)

    def pen_rows(g, c):
        return [pen_ref[g, i, pl.ds(c * chunk + kb, 1), :] for kb in range(chunk)]

    def logits(g, c, slot):
        s_ref, smax_ref = slots[slot]
        k0 = pl.multiple_of(c * cw, cw)
        s = lax.dot_general(k_ref[g, pl.ds(k0, cw), :], q_ref[g, pl.ds(q0, cw), :],
                            nt_dims, preferred_element_type=F32)
        pen = pen_rows(g, c)
        for qb in range(chunk):
            cols = slice(qb * blk, (qb + 1) * blk)
            j0 = jnp.maximum(cap + (chunk - 1) - ((i - c) * chunk + qb), 0)
            sq = s[:, cols] + bias_ref[g, pl.ds(j0, chunk)].reshape(cw, blk)
            s_ref[g, :, cols] = sq
            block_max = [jnp.max(sq[kb * blk:(kb + 1) * blk, :], axis=0, keepdims=True)
                         + pen[kb][:, cols] for kb in range(chunk)]
            smax_ref[g, :, cols] = functools.reduce(jnp.maximum, block_max)

    def softmax_step(g, c, slot):
        s_ref, smax_ref = slots[slot]
        m = m_ref[g]
        m_new = jnp.maximum(m, smax_ref[g])
        alpha = jnp.exp2(m - m_new)
        p = jnp.concatenate(
            [jnp.exp2(s_ref[g, kb * blk:(kb + 1) * blk, :] - jnp.where(pen_kb < 0.0, -NEG_INF, m_new))
             for kb, pen_kb in enumerate(pen_rows(g, c))], axis=0)
        m_ref[g] = m_new
        acc_ref[g] = alpha * acc_ref[g] + jnp.dot(vt_ref[g, c], p.astype(BF16),
                                                   preferred_element_type=F32)

    def step(c, slot):
        for g in range(n_heads):
            logits(g, c + 1, 1 - slot)
            softmax_step(g, c, slot)

    slots = ((s0_ref, smax0_ref), (s1_ref, smax1_ref))
    odd = i % 2
    for g in range(n_heads):
        m_ref[g] = jnp.full((1, cw), NEG_INF, F32)
        acc_ref[g] = jnp.zeros(acc_ref.shape[1:], F32)

    @pl.when(odd == 0)
    def _():
        for g in range(n_heads):
            logits(g, 0, 0)

    @pl.when(odd == 1)
    def _():
        for g in range(n_heads):
            logits(g, 0, 1)
        step(0, 1)

    def pair(j, carry):
        c = odd + 2 * j
        step(c, 0)
        step(c + 1, 1)
        return carry

    lax.fori_loop(0, i // 2, pair, 0)
    for g in range(n_heads):
        softmax_step(g, i, 0)
        acc = acc_ref[g]
        out = acc[:hd, :] * (1.0 / acc[hd:hd + 1, :])
        o_ref[:, g * hd:(g + 1) * hd] = out.T.astype(BF16)


def _moba_attention(qkv, rel_bias, batch):
    _, t, hd = qkv.shape
    s = t // batch
    blk = MOBA_BLOCK
    nb = s // blk
    g, chunk = ATTN_HEADS_PER_STEP, ATTN_CHUNK_BLOCKS
    groups = N_HEADS // g
    cap = _bias_tile_cap(nb)
    n_tiles = cap + 2 * chunk - 1
    dl = jnp.minimum(cap + (chunk - 1) - jnp.arange(n_tiles, dtype=jnp.int32), cap)[:, None]
    n = jnp.arange(2 * blk, dtype=jnp.int32)[None, :]
    buckets = _t5_bucket(dl * blk + n - (blk - 1))
    return pl.pallas_call(
        _attn_kernel,
        grid=(groups, batch, nb // chunk),
        in_specs=[
            pl.BlockSpec(memory_space=pltpu.SMEM),
            pl.BlockSpec((n_tiles, 2 * blk), lambda hg, b, i: (0, 0)),
            pl.BlockSpec((g, s, hd), lambda hg, b, i: (hg, b, 0)),
            pl.BlockSpec((g, s, hd), lambda hg, b, i: (groups + hg, b, 0)),
            pl.BlockSpec((g, s, hd), lambda hg, b, i: (2 * groups + hg, b, 0)),
        ],
        out_specs=pl.BlockSpec((chunk * blk, g * hd), lambda hg, b, i: (b * (nb // chunk) + i, hg)),
        out_shape=jax.ShapeDtypeStruct((t, N_HEADS * hd), BF16),
        scratch_shapes=[
            pltpu.VMEM((g, n_tiles, blk, blk), F32),
            pltpu.VMEM((g, s // (chunk * blk), nb, chunk * blk), F32),
            pltpu.VMEM((g, s // (chunk * blk), hd + ATTN_SUM_ROWS, chunk * blk), BF16),
            pltpu.VMEM((g, chunk * blk, chunk * blk), F32),
            pltpu.VMEM((g, chunk * blk, chunk * blk), F32),
            pltpu.VMEM((g, 1, chunk * blk), F32),
            pltpu.VMEM((g, 1, chunk * blk), F32),
            pltpu.VMEM((g, 1, chunk * blk), F32),
            pltpu.VMEM((g, hd + ATTN_SUM_ROWS, chunk * blk), F32),
        ],
        compiler_params=_params("arbitrary", "arbitrary", "arbitrary"),
        name="moba_attention",
    )(rel_bias.T.astype(F32), buckets, qkv, qkv, qkv)


def _wo_kernel(a_ref, w_ref, x_ref, o_ref):
    o_ref[...] = x_ref[...] + jnp.dot(a_ref[...], w_ref[...], preferred_element_type=F32)


def _wo_proj(a, x, w, attn_layer):
    t, d = x.shape
    tm = OUT_TM
    return pl.pallas_call(
        _wo_kernel,
        grid=(t // tm,),
        in_specs=[
            pl.BlockSpec((tm, d), lambda i: (i, 0)),
            pl.BlockSpec((None, d, d), lambda i: (attn_layer, 0, 0)),
            pl.BlockSpec((tm, d), lambda i: (i, 0)),
        ],
        out_specs=pl.BlockSpec((tm, d), lambda i: (i, 0)),
        out_shape=jax.ShapeDtypeStruct((t, d), F32),
        compiler_params=_params("parallel"),
        name="attn_wo",
    )(a, w, x)


def kernel(x, rel_bias, ffn1_norm, ffn1_w_gate, ffn1_w_up, ffn1_w_down, mix_norm, ffn2_norm, ffn2_w_gate, ffn2_w_up, ffn2_w_down, conv_pw1_w, conv_pw1_b, conv_dw_w, conv_dw_b, conv_ln_g, conv_ln_b, conv_pw2_w, conv_pw2_b, attn_wqkv, attn_q_norm, attn_k_norm, attn_wo):
    batch, seq, d = x.shape
    depth = ffn1_norm.shape[0]
    n_slabs = d // LANES
    bf = lambda w: w.astype(BF16)
    vec = lambda v: v.reshape(v.shape[0], 1, v.shape[-1])
    ffn1 = (vec(ffn1_norm), ffn1_w_gate, ffn1_w_up, ffn1_w_down)
    ffn2 = (vec(ffn2_norm), ffn2_w_gate, ffn2_w_up, ffn2_w_down)
    mix_g = vec(mix_norm)
    pw1_w, pw1_b = conv_pw1_w, vec(conv_pw1_b)
    nc = conv_dw_w.shape[0]
    dw_w = conv_dw_w.reshape(nc, CONV_WIDTH, n_slabs, LANES).transpose(0, 2, 1, 3)
    dw_b = conv_dw_b.reshape(nc, n_slabs, 1, LANES)
    pw2_w = bf(conv_pw2_w)
    wqkv, wo = attn_wqkv, bf(attn_wo)

    xt = x.reshape(batch * seq, d)
    for i in range(depth):
        xt = _ffn(xt, *ffn1, i)
        j = i // N_MIXERS
        if i % N_MIXERS == 0:
            u = _pw1_glu(xt, mix_g, pw1_w, pw1_b, i, j)
            xt = _conv_module(u, xt, dw_w, dw_b, vec(conv_ln_g), vec(conv_ln_b),
                              pw2_w, vec(conv_pw2_b), j, batch)
        else:
            qkv = _qkv_proj(xt, mix_g, wqkv, vec(attn_q_norm), vec(attn_k_norm), i, j)
            a = _moba_attention(qkv, rel_bias, batch)
            xt = _wo_proj(a, xt, wo, j)
        xt = _ffn(xt, *ffn2, i)
    return xt.reshape(batch, seq, d)
```

```python
import functools
import math

import jax
import jax.numpy as jnp
from jax import lax
from jax.experimental import pallas as pl
from jax.experimental.pallas import tpu as pltpu

D_MODEL = 2048
N_HEADS = 16
HEAD_DIM = D_MODEL // N_HEADS
D_FF = 5632
CONV_WIDTH = 31
MOBA_BLOCK = 256
MOBA_TOPK = 3
NUM_BUCKETS = 32
MAX_DISTANCE = 2048
NORM_EPS = 1e-6
NEG_INF = -1e30
N_MIXERS = 2

LANES = 128
SUBLANES = 8
VMEM_LIMIT_BYTES = 60000 * 1024

BF16 = jnp.bfloat16
F32 = jnp.float32

FFN_TM = 1024
FFN_TF = 512
PROJ_TM = 1024
PROJ_TN = 512
QKV_TN = 1024
OUT_TM = 512
CONV_TS = 256
CONV_HALO = 32
CONV_ROWS = 64
CONV_PW2_COLS = 256
ATTN_HEADS_PER_STEP = 4
ATTN_CHUNK_BLOCKS = 2
ATTN_SUM_ROWS = 16
LOG2E = math.log2(math.e)


def _params(*semantics):
    return pltpu.CompilerParams(
        dimension_semantics=semantics, vmem_limit_bytes=VMEM_LIMIT_BYTES)


def _rms_norm_rows(x, g):
    ms = jnp.mean(x * x, axis=-1, keepdims=True)
    return (x * lax.rsqrt(ms + NORM_EPS)) * g


def _sigmoid(x):
    return 1.0 / (1.0 + jnp.exp(-x))


def _ffn_kernel(x_hbm, g_ref, wg_ref, wu_ref, wd_ref, o_ref, x_buf, h_ref, x_sem):
    i = pl.program_id(0)
    f = pl.program_id(1)
    tm = x_buf.shape[0]

    def x_copy(tile):
        rows = pl.ds(pl.multiple_of(tile * tm, tm), tm)
        return pltpu.make_async_copy(x_hbm.at[rows, :], x_buf, x_sem)

    @pl.when((i == 0) & (f == 0))
    def _():
        x_copy(0).start()

    @pl.when(f == 0)
    def _():
        x_copy(i).wait()
        x = x_buf[...]
        h_ref[...] = _rms_norm_rows(x, g_ref[...]).astype(BF16)
        o_ref[...] = x

    @pl.when((f == 1) & (i + 1 < pl.num_programs(0)))
    def _():
        x_copy(i + 1).start()

    h = h_ref[...]
    gate = jnp.dot(h, wg_ref[...].astype(BF16), preferred_element_type=F32)
    up = jnp.dot(h, wu_ref[...].astype(BF16), preferred_element_type=F32)
    act = (gate * _sigmoid(gate)) * up * 0.5
    o_ref[...] += jnp.dot(act.astype(BF16), wd_ref[...].astype(BF16),
                          preferred_element_type=F32)


def _ffn(x, norm_g, w_gate, w_up, w_down, layer):
    t, d = x.shape
    f = w_gate.shape[-1]
    tm, tf = FFN_TM, FFN_TF
    assert f // tf >= 2
    return pl.pallas_call(
        _ffn_kernel,
        grid=(t // tm, f // tf),
        in_specs=[
            pl.BlockSpec(memory_space=pl.ANY),
            pl.BlockSpec((None, 1, d), lambda i, j: (layer, 0, 0)),
            pl.BlockSpec((None, d, tf), lambda i, j: (layer, 0, j)),
            pl.BlockSpec((None, d, tf), lambda i, j: (layer, 0, j)),
            pl.BlockSpec((None, tf, d), lambda i, j: (layer, j, 0)),
        ],
        out_specs=pl.BlockSpec((tm, d), lambda i, j: (i, 0)),
        out_shape=jax.ShapeDtypeStruct((t, d), F32),
        scratch_shapes=[
            pltpu.VMEM((tm, d), F32),
            pltpu.VMEM((tm, d), BF16),
            pltpu.SemaphoreType.DMA(()),
        ],
        compiler_params=_params("arbitrary", "arbitrary"),
        name="ffn",
    )(x, norm_g, w_gate, w_up, w_down)


def _pw1_kernel(x_ref, g_ref, wa_ref, wg_ref, ba_ref, bg_ref, o_ref, h_ref):
    @pl.when(pl.program_id(1) == 0)
    def _():
        h_ref[...] = _rms_norm_rows(x_ref[...], g_ref[...]).astype(BF16)

    h = h_ref[...]
    a = jnp.dot(h, wa_ref[...].astype(BF16), preferred_element_type=F32) + ba_ref[...]
    g = jnp.dot(h, wg_ref[...].astype(BF16), preferred_element_type=F32) + bg_ref[...]
    o_ref[...] = a * _sigmoid(g)


def _pw1_glu(x, norm_g, w, b, layer, conv_layer):
    t, d = x.shape
    tm, tn = PROJ_TM, PROJ_TN
    nj = d // tn
    return pl.pallas_call(
        _pw1_kernel,
        grid=(t // tm, nj),
        in_specs=[
            pl.BlockSpec((tm, d), lambda i, j: (i, 0)),
            pl.BlockSpec((None, 1, d), lambda i, j: (layer, 0, 0)),
            pl.BlockSpec((None, d, tn), lambda i, j: (conv_layer, 0, j)),
            pl.BlockSpec((None, d, tn), lambda i, j: (conv_layer, 0, j + nj)),
            pl.BlockSpec((None, 1, tn), lambda i, j: (conv_layer, 0, j)),
            pl.BlockSpec((None, 1, tn), lambda i, j: (conv_layer, 0, j + nj)),
        ],
        out_specs=pl.BlockSpec((tm, tn), lambda i, j: (i, j)),
        out_shape=jax.ShapeDtypeStruct((t, d), F32),
        scratch_shapes=[pltpu.VMEM((tm, d), BF16)],
        compiler_params=_params("parallel", "arbitrary"),
        name="conv_pw1_glu",
    )(x, norm_g, w, w, b, b)


def _conv_kernel(tiles_per_seq, u_ref, halo_ref, dww_ref, dwb_ref, lng_ref, lnb_ref, w2_ref,
                 b2_ref, x_ref, o_ref, ext_ref, cv_ref, y_ref):
    ts, d = u_ref.shape
    n_slabs = d // LANES
    pad = CONV_HALO - (CONV_WIDTH - 1)
    s = pl.program_id(0)
    conv_tile = jnp.minimum(s, pl.num_programs(0) - 2)
    first = lax.rem(conv_tile, tiles_per_seq) == 0

    @pl.when(s == 0)
    def _():
        y_ref[...] = jnp.zeros(y_ref.shape, BF16)

    for c in range(n_slabs):
        lanes = slice(c * LANES, (c + 1) * LANES)
        halo = halo_ref[:, lanes]
        ext_ref[c, 0:CONV_HALO, :] = jnp.where(first, 0.0, halo)
        ext_ref[c, CONV_HALO:, :] = u_ref[:, lanes]

    slabs_per_piece = CONV_PW2_COLS // LANES
    for piece in range(d // CONV_PW2_COLS):
        for c in range(piece * slabs_per_piece, (piece + 1) * slabs_per_piece):
            w_rows = [dww_ref[c, k:k + 1, :] for k in range(CONV_WIDTH)]
            bias = dwb_ref[c]
            for r0 in range(0, ts, CONV_ROWS):
                acc = jnp.broadcast_to(bias, (CONV_ROWS, LANES))
                for k in range(CONV_WIDTH):
                    acc = acc + w_rows[k] * ext_ref[c, r0 + pad + k:r0 + pad + k + CONV_ROWS, :]
                cv_ref[c, r0:r0 + CONV_ROWS, :] = acc
        cols = slice(piece * CONV_PW2_COLS, (piece + 1) * CONV_PW2_COLS)
        o_ref[:, cols] = (x_ref[:, cols] + b2_ref[:, cols]
                          + jnp.dot(y_ref[...], w2_ref[:, cols], preferred_element_type=F32))

    total = cv_ref[0]
    for c in range(1, n_slabs):
        total = total + cv_ref[c]
    mu = jnp.sum(total, axis=-1, keepdims=True) * (1.0 / d)
    sq = jnp.zeros((ts, LANES), F32)
    for c in range(n_slabs):
        xc = cv_ref[c] - mu
        sq = sq + xc * xc
    rstd = lax.rsqrt(jnp.sum(sq, axis=-1, keepdims=True) * (1.0 / d) + NORM_EPS)
    for c in range(n_slabs):
        lanes = slice(c * LANES, (c + 1) * LANES)
        y = (cv_ref[c] - mu) * rstd * lng_ref[:, lanes] + lnb_ref[:, lanes]
        y_ref[:, lanes] = (y * _sigmoid(y)).astype(BF16)


def _conv_module(u, x, dw_w, dw_b, ln_g, ln_b, w2, b2, conv_layer, batch):
    t, d = x.shape
    ts = CONV_TS
    n_slabs = d // LANES
    n_tiles = t // ts
    tiles_per_seq = t // batch // ts
    hpt = ts // CONV_HALO

    conv_row = lambda s: (jnp.minimum(s, n_tiles - 1), 0)
    halo_row = lambda s: (jnp.maximum(jnp.minimum(s, n_tiles - 1) * hpt - 1, 0), 0)
    pw2_row = lambda s: (jnp.maximum(s - 1, 0), 0)
    vec = lambda s: (conv_layer, 0, 0)
    return pl.pallas_call(
        functools.partial(_conv_kernel, tiles_per_seq),
        grid=(n_tiles + 1,),
        in_specs=[
            pl.BlockSpec((ts, d), conv_row),
            pl.BlockSpec((CONV_HALO, d), halo_row),
            pl.BlockSpec((None, n_slabs, CONV_WIDTH, LANES), lambda s: (conv_layer, 0, 0, 0)),
            pl.BlockSpec((None, n_slabs, 1, LANES), lambda s: (conv_layer, 0, 0, 0)),
            pl.BlockSpec((None, 1, d), vec),
            pl.BlockSpec((None, 1, d), vec),
            pl.BlockSpec((None, d, d), vec),
            pl.BlockSpec((None, 1, d), vec),
            pl.BlockSpec((ts, d), pw2_row),
        ],
        out_specs=pl.BlockSpec((ts, d), pw2_row),
        out_shape=jax.ShapeDtypeStruct((t, d), F32),
        scratch_shapes=[
            pltpu.VMEM((n_slabs, ts + CONV_HALO, LANES), F32),
            pltpu.VMEM((n_slabs, ts, LANES), F32),
            pltpu.VMEM((ts, d), BF16),
        ],
        compiler_params=_params("arbitrary"),
        name="conv_dw_ln_pw2",
    )(u, u, dw_w, dw_b, ln_g, ln_b, w2, b2, x)


def _qkv_kernel(x_ref, g_ref, w_ref, qn_ref, kn_ref, o_ref, h_ref):
    j = pl.program_id(1)
    heads_per_step = o_ref.shape[0]
    n_q_steps = N_HEADS // heads_per_step

    @pl.when(j == 0)
    def _():
        h_ref[...] = _rms_norm_rows(x_ref[...], g_ref[...]).astype(BF16)

    y = jnp.dot(h_ref[...], w_ref[...].astype(BF16), preferred_element_type=F32)
    is_q = j < n_q_steps
    is_v = j >= 2 * n_q_steps
    gain = jnp.where(is_q, qn_ref[...] * (HEAD_DIM ** -0.5 * LOG2E), kn_ref[...])
    for c in range(heads_per_step):
        t = y[:, c * HEAD_DIM:(c + 1) * HEAD_DIM]
        normed = _rms_norm_rows(t, gain)
        o_ref[c] = jnp.where(is_v, t, normed).astype(BF16)


def _qkv_proj(x, norm_g, w, q_norm, k_norm, layer, attn_layer):
    t, d = x.shape
    tm, tn = PROJ_TM, QKV_TN
    hps = tn // HEAD_DIM
    return pl.pallas_call(
        _qkv_kernel,
        grid=(t // tm, 3 * d // tn),
        in_specs=[
            pl.BlockSpec((tm, d), lambda i, j: (i, 0)),
            pl.BlockSpec((None, 1, d), lambda i, j: (layer, 0, 0)),
            pl.BlockSpec((None, d, tn), lambda i, j: (attn_layer, 0, j)),
            pl.BlockSpec((None, 1, HEAD_DIM), lambda i, j: (attn_layer, 0, 0)),
            pl.BlockSpec((None, 1, HEAD_DIM), lambda i, j: (attn_layer, 0, 0)),
        ],
        out_specs=pl.BlockSpec((hps, tm, HEAD_DIM), lambda i, j: (j, i, 0)),
        out_shape=jax.ShapeDtypeStruct((3 * N_HEADS, t, HEAD_DIM), BF16),
        scratch_shapes=[pltpu.VMEM((tm, d), BF16)],
        compiler_params=_params("parallel", "arbitrary"),
        name="attn_qkv",
    )(x, norm_g, w, q_norm, k_norm)


def _t5_bucket(dist):
    max_exact = NUM_BUCKETS // 2
    n = jnp.maximum(dist, 0)
    nf = jnp.maximum(n, 1).astype(F32)
    large = max_exact + (jnp.log(nf / max_exact) / math.log(MAX_DISTANCE / max_exact)
                         * (NUM_BUCKETS - max_exact)).astype(jnp.int32)
    large = jnp.minimum(large, NUM_BUCKETS - 1)
    return jnp.where(n < max_exact, n, large)


def _bias_tile_cap(n_blocks):
    max_exact = NUM_BUCKETS // 2
    d = max_exact
    while max_exact + int(math.log(d / max_exact) / math.log(MAX_DISTANCE / max_exact)
                          * (NUM_BUCKETS - max_exact)) < NUM_BUCKETS - 1:
        d += 1
    cap = -(-(d + MOBA_BLOCK - 1) // MOBA_BLOCK)
    return min(cap, n_blocks - 1)


def _attn_kernel(tbl_ref, bucket_ref, q_ref, k_ref, v_ref, o_ref,
                 bias_ref, pen_ref, vt_ref, s0_ref, s1_ref, smax0_ref, smax1_ref,
                 m_ref, acc_ref):
    hg = pl.program_id(0)
    b = pl.program_id(1)
    i = pl.program_id(2)
    blk = MOBA_BLOCK
    n_heads, s_len, hd = q_ref.shape
    nb = s_len // blk
    chunk = ATTN_CHUNK_BLOCKS
    cw = chunk * blk
    n_tiles = bias_ref.shape[1]
    cap = n_tiles - (2 * chunk - 1)
    nt_dims = (((1,), (1,)), ((), ()))

    @pl.when((b == 0) & (i == 0))
    def _():
        x_idx = lax.broadcasted_iota(jnp.int32, (blk, blk), 0)
        y_idx = lax.broadcasted_iota(jnp.int32, (blk, blk), 1)
        for g in range(n_heads):
            h = hg * n_heads + g
            for j in range(n_tiles):
                dl = min(cap + chunk - 1 - j, cap)
                if dl < 0:
                    bias_ref[g, j] = jnp.full((blk, blk), NEG_INF, F32)
                    continue
                bk = bucket_ref[j:j + 1, :]
                w = jnp.zeros(bk.shape, F32)
                for bucket in range(NUM_BUCKETS):
                    w = jnp.where(bk == bucket, tbl_ref[h, bucket], w)
                rows = jnp.broadcast_to(w * LOG2E, (blk, 2 * blk))
                tile = pltpu.roll(rows, blk + 1, 1, stride=1, stride_axis=0)[:, :blk]
                if dl == 0:
                    tile = jnp.where(x_idx <= y_idx, tile, NEG_INF)
                bias_ref[g, j] = tile

    @pl.when(i == 0)
    def _():
        for g in range(n_heads):
            q = q_ref[g]
            k = k_ref[g]
            kmean = jnp.sum(k.astype(F32).reshape(nb, blk, hd), axis=1) * (1.0 / blk)
            km_hi = kmean.astype(BF16)
            km_lo = (kmean - km_hi.astype(F32)).astype(BF16)
            gate = (lax.dot_general(km_hi, q, nt_dims, preferred_element_type=F32)
                    + lax.dot_general(km_lo, q, nt_dims, preferred_element_type=F32))
            n_idx = lax.broadcasted_iota(jnp.int32, gate.shape, 0)
            own = lax.shift_right_logical(
                lax.broadcasted_iota(jnp.int32, gate.shape, 1), int(math.log2(blk)))
            cnt = jnp.zeros(gate.shape, jnp.int32)
            for n in range(nb):
                row = gate[n:n + 1, :]
                beats = jnp.where(row > gate, 1, jnp.where((row == gate) & (n < n_idx), 1, 0))
                cnt = cnt + jnp.where(n < own, beats, 0)
            keep = ((n_idx < own) & (cnt < MOBA_TOPK)) | (n_idx == own)
            pen = jnp.where(keep, 0.0, NEG_INF)
            for t in range(s_len // cw):
                pen_ref[g, t] = pen[:, t * cw:(t + 1) * cw]
            v = v_ref[g].astype(F32)
            ones_row = (lax.broadcasted_iota(jnp.int32, (ATTN_SUM_ROWS, cw), 0) == 0)
            for c in range(s_len // cw):
                vt_ref[g, c, :hd, :] = v[c * cw:(c + 1) * cw, :].T.astype(BF16)
                vt_ref[g, c, hd:, :] = jnp.where(ones_row, 1.0, 0.0).astype(BF16)

    q0 = pl.multiple_of(i * cw, cw)

    def pen_rows(g, c):
        return [pen_ref[g, i, pl.ds(c * chunk + kb, 1), :] for kb in range(chunk)]

    def logits(g, c, slot):
        s_ref, smax_ref = slots[slot]
        k0 = pl.multiple_of(c * cw, cw)
        s = lax.dot_general(k_ref[g, pl.ds(k0, cw), :], q_ref[g, pl.ds(q0, cw), :],
                            nt_dims, preferred_element_type=F32)
        pen = pen_rows(g, c)
        for qb in range(chunk):
            cols = slice(qb * blk, (qb + 1) * blk)
            j0 = jnp.maximum(cap + (chunk - 1) - ((i - c) * chunk + qb), 0)
            sq = s[:, cols] + bias_ref[g, pl.ds(j0, chunk)].reshape(cw, blk)
            s_ref[g, :, cols] = sq
            block_max = [jnp.max(sq[kb * blk:(kb + 1) * blk, :], axis=0, keepdims=True)
                         + pen[kb][:, cols] for kb in range(chunk)]
            smax_ref[g, :, cols] = functools.reduce(jnp.maximum, block_max)

    def softmax_step(g, c, slot):
        s_ref, smax_ref = slots[slot]
        m = m_ref[g]
        m_new = jnp.maximum(m, smax_ref[g])
        alpha = jnp.exp2(m - m_new)
        p = jnp.concatenate(
            [jnp.exp2(s_ref[g, kb * blk:(kb + 1) * blk, :] - jnp.where(pen_kb < 0.0, -NEG_INF, m_new))
             for kb, pen_kb in enumerate(pen_rows(g, c))], axis=0)
        m_ref[g] = m_new
        acc_ref[g] = alpha * acc_ref[g] + jnp.dot(vt_ref[g, c], p.astype(BF16),
                                                   preferred_element_type=F32)

    def step(c, slot):
        for g in range(n_heads):
            logits(g, c + 1, 1 - slot)
            softmax_step(g, c, slot)

    slots = ((s0_ref, smax0_ref), (s1_ref, smax1_ref))
    odd = i % 2
    for g in range(n_heads):
        m_ref[g] = jnp.full((1, cw), NEG_INF, F32)
        acc_ref[g] = jnp.zeros(acc_ref.shape[1:], F32)

    @pl.when(odd == 0)
    def _():
        for g in range(n_heads):
            logits(g, 0, 0)

    @pl.when(odd == 1)
    def _():
        for g in range(n_heads):
            logits(g, 0, 1)
        step(0, 1)

    def pair(j, carry):
        c = odd + 2 * j
        step(c, 0)
        step(c + 1, 1)
        return carry

    lax.fori_loop(0, i // 2, pair, 0)
    for g in range(n_heads):
        softmax_step(g, i, 0)
        acc = acc_ref[g]
        out = acc[:hd, :] * (1.0 / acc[hd:hd + 1, :])
        o_ref[:, g * hd:(g + 1) * hd] = out.T.astype(BF16)


def _moba_attention(qkv, rel_bias, batch):
    _, t, hd = qkv.shape
    s = t // batch
    blk = MOBA_BLOCK
    nb = s // blk
    g, chunk = ATTN_HEADS_PER_STEP, ATTN_CHUNK_BLOCKS
    groups = N_HEADS // g
    cap = _bias_tile_cap(nb)
    n_tiles = cap + 2 * chunk - 1
    dl = jnp.minimum(cap + (chunk - 1) - jnp.arange(n_tiles, dtype=jnp.int32), cap)[:, None]
    n = jnp.arange(2 * blk, dtype=jnp.int32)[None, :]
    buckets = _t5_bucket(dl * blk + n - (blk - 1))
    return pl.pallas_call(
        _attn_kernel,
        grid=(groups, batch, nb // chunk),
        in_specs=[
            pl.BlockSpec(memory_space=pltpu.SMEM),
            pl.BlockSpec((n_tiles, 2 * blk), lambda hg, b, i: (0, 0)),
            pl.BlockSpec((g, s, hd), lambda hg, b, i: (hg, b, 0)),
            pl.BlockSpec((g, s, hd), lambda hg, b, i: (groups + hg, b, 0)),
            pl.BlockSpec((g, s, hd), lambda hg, b, i: (2 * groups + hg, b, 0)),
        ],
        out_specs=pl.BlockSpec((chunk * blk, g * hd), lambda hg, b, i: (b * (nb // chunk) + i, hg)),
        out_shape=jax.ShapeDtypeStruct((t, N_HEADS * hd), BF16),
        scratch_shapes=[
            pltpu.VMEM((g, n_tiles, blk, blk), F32),
            pltpu.VMEM((g, s // (chunk * blk), nb, chunk * blk), F32),
            pltpu.VMEM((g, s // (chunk * blk), hd + ATTN_SUM_ROWS, chunk * blk), BF16),
            pltpu.VMEM((g, chunk * blk, chunk * blk), F32),
            pltpu.VMEM((g, chunk * blk, chunk * blk), F32),
            pltpu.VMEM((g, 1, chunk * blk), F32),
            pltpu.VMEM((g, 1, chunk * blk), F32),
            pltpu.VMEM((g, 1, chunk * blk), F32),
            pltpu.VMEM((g, hd + ATTN_SUM_ROWS, chunk * blk), F32),
        ],
        compiler_params=_params("arbitrary", "arbitrary", "arbitrary"),
        name="moba_attention",
    )(rel_bias.T.astype(F32), buckets, qkv, qkv, qkv)


def _wo_kernel(a_ref, w_ref, x_ref, o_ref):
    o_ref[...] = x_ref[...] + jnp.dot(a_ref[...], w_ref[...], preferred_element_type=F32)


def _wo_proj(a, x, w, attn_layer):
    t, d = x.shape
    tm = OUT_TM
    return pl.pallas_call(
        _wo_kernel,
        grid=(t // tm,),
        in_specs=[
            pl.BlockSpec((tm, d), lambda i: (i, 0)),
            pl.BlockSpec((None, d, d), lambda i: (attn_layer, 0, 0)),
            pl.BlockSpec((tm, d), lambda i: (i, 0)),
        ],
        out_specs=pl.BlockSpec((tm, d), lambda i: (i, 0)),
        out_shape=jax.ShapeDtypeStruct((t, d), F32),
        compiler_params=_params("parallel"),
        name="attn_wo",
    )(a, w, x)


def kernel(x, rel_bias, ffn1_norm, ffn1_w_gate, ffn1_w_up, ffn1_w_down, mix_norm, ffn2_norm, ffn2_w_gate, ffn2_w_up, ffn2_w_down, conv_pw1_w, conv_pw1_b, conv_dw_w, conv_dw_b, conv_ln_g, conv_ln_b, conv_pw2_w, conv_pw2_b, attn_wqkv, attn_q_norm, attn_k_norm, attn_wo):
    batch, seq, d = x.shape
    depth = ffn1_norm.shape[0]
    n_slabs = d // LANES
    bf = lambda w: w.astype(BF16)
    vec = lambda v: v.reshape(v.shape[0], 1, v.shape[-1])
    ffn1 = (vec(ffn1_norm), ffn1_w_gate, ffn1_w_up, ffn1_w_down)
    ffn2 = (vec(ffn2_norm), ffn2_w_gate, ffn2_w_up, ffn2_w_down)
    mix_g = vec(mix_norm)
    pw1_w, pw1_b = conv_pw1_w, vec(conv_pw1_b)
    nc = conv_dw_w.shape[0]
    dw_w = conv_dw_w.reshape(nc, CONV_WIDTH, n_slabs, LANES).transpose(0, 2, 1, 3)
    dw_b = conv_dw_b.reshape(nc, n_slabs, 1, LANES)
    pw2_w = bf(conv_pw2_w)
    wqkv, wo = attn_wqkv, bf(attn_wo)

    xt = x.reshape(batch * seq, d)
    for i in range(depth):
        xt = _ffn(xt, *ffn1, i)
        j = i // N_MIXERS
        if i % N_MIXERS == 0:
            u = _pw1_glu(xt, mix_g, pw1_w, pw1_b, i, j)
            xt = _conv_module(u, xt, dw_w, dw_b, vec(conv_ln_g), vec(conv_ln_b),
                              pw2_w, vec(conv_pw2_b), j, batch)
        else:
            qkv = _qkv_proj(xt, mix_g, wqkv, vec(attn_q_norm), vec(attn_k_norm), i, j)
            a = _moba_attention(qkv, rel_bias, batch)
            xt = _wo_proj(a, xt, wo, j)
        xt = _ffn(xt, *ffn2, i)
    return xt.reshape(batch, seq, d)
```

```python
import functools
import math

import jax
import jax.numpy as jnp
from jax import lax
from jax.experimental import pallas as pl
from jax.experimental.pallas import tpu as pltpu

D_MODEL = 2048
N_HEADS = 16
HEAD_DIM = D_MODEL // N_HEADS
D_FF = 5632
CONV_WIDTH = 31
MOBA_BLOCK = 256
MOBA_TOPK = 3
NUM_BUCKETS = 32
MAX_DISTANCE = 2048
NORM_EPS = 1e-6
NEG_INF = -1e30
N_MIXERS = 2

LANES = 128
SUBLANES = 8
VMEM_LIMIT_BYTES = 60000 * 1024

BF16 = jnp.bfloat16
F32 = jnp.float32

FFN_TM = 1024
FFN_TF = 512
PROJ_TM = 1024
PROJ_TN = 512
QKV_TN = 1024
OUT_TM = 512
CONV_TS = 256
CONV_HALO = 32
CONV_ROWS = 128
CONV_PW2_COLS = 256
ATTN_HEADS_PER_STEP = 4
ATTN_CHUNK_BLOCKS = 2
ATTN_SUM_ROWS = 16
LOG2E = math.log2(math.e)


def _params(*semantics):
    return pltpu.CompilerParams(
        dimension_semantics=semantics, vmem_limit_bytes=VMEM_LIMIT_BYTES)


def _rms_norm_rows(x, g):
    ms = jnp.mean(x * x, axis=-1, keepdims=True)
    return (x * lax.rsqrt(ms + NORM_EPS)) * g


def _sigmoid(x):
    return 1.0 / (1.0 + jnp.exp(-x))


def _ffn_kernel(x_hbm, g_ref, wg_ref, wu_ref, wd_ref, o_ref, x_buf, h_ref, x_sem):
    i = pl.program_id(0)
    f = pl.program_id(1)
    tm = x_buf.shape[0]

    def x_copy(tile):
        rows = pl.ds(pl.multiple_of(tile * tm, tm), tm)
        return pltpu.make_async_copy(x_hbm.at[rows, :], x_buf, x_sem)

    @pl.when((i == 0) & (f == 0))
    def _():
        x_copy(0).start()

    @pl.when(f == 0)
    def _():
        x_copy(i).wait()
        x = x_buf[...]
        h_ref[...] = _rms_norm_rows(x, g_ref[...]).astype(BF16)
        o_ref[...] = x

    @pl.when((f == 1) & (i + 1 < pl.num_programs(0)))
    def _():
        x_copy(i + 1).start()

    h = h_ref[...]
    gate = jnp.dot(h, wg_ref[...].astype(BF16), preferred_element_type=F32)
    up = jnp.dot(h, wu_ref[...].astype(BF16), preferred_element_type=F32)
    act = (gate * _sigmoid(gate)) * up * 0.5
    o_ref[...] += jnp.dot(act.astype(BF16), wd_ref[...].astype(BF16),
                          preferred_element_type=F32)


def _ffn(x, norm_g, w_gate, w_up, w_down, layer):
    t, d = x.shape
    f = w_gate.shape[-1]
    tm, tf = FFN_TM, FFN_TF
    assert f // tf >= 2
    return pl.pallas_call(
        _ffn_kernel,
        grid=(t // tm, f // tf),
        in_specs=[
            pl.BlockSpec(memory_space=pl.ANY),
            pl.BlockSpec((None, 1, d), lambda i, j: (layer, 0, 0)),
            pl.BlockSpec((None, d, tf), lambda i, j: (layer, 0, j)),
            pl.BlockSpec((None, d, tf), lambda i, j: (layer, 0, j)),
            pl.BlockSpec((None, tf, d), lambda i, j: (layer, j, 0)),
        ],
        out_specs=pl.BlockSpec((tm, d), lambda i, j: (i, 0)),
        out_shape=jax.ShapeDtypeStruct((t, d), F32),
        scratch_shapes=[
            pltpu.VMEM((tm, d), F32),
            pltpu.VMEM((tm, d), BF16),
            pltpu.SemaphoreType.DMA(()),
        ],
        compiler_params=_params("arbitrary", "arbitrary"),
        name="ffn",
    )(x, norm_g, w_gate, w_up, w_down)


def _pw1_kernel(x_ref, g_ref, wa_ref, wg_ref, ba_ref, bg_ref, o_ref, h_ref):
    @pl.when(pl.program_id(1) == 0)
    def _():
        h_ref[...] = _rms_norm_rows(x_ref[...], g_ref[...]).astype(BF16)

    h = h_ref[...]
    a = jnp.dot(h, wa_ref[...].astype(BF16), preferred_element_type=F32) + ba_ref[...]
    g = jnp.dot(h, wg_ref[...].astype(BF16), preferred_element_type=F32) + bg_ref[...]
    o_ref[...] = a * _sigmoid(g)


def _pw1_glu(x, norm_g, w, b, layer, conv_layer):
    t, d = x.shape
    tm, tn = PROJ_TM, PROJ_TN
    nj = d // tn
    return pl.pallas_call(
        _pw1_kernel,
        grid=(t // tm, nj),
        in_specs=[
            pl.BlockSpec((tm, d), lambda i, j: (i, 0)),
            pl.BlockSpec((None, 1, d), lambda i, j: (layer, 0, 0)),
            pl.BlockSpec((None, d, tn), lambda i, j: (conv_layer, 0, j)),
            pl.BlockSpec((None, d, tn), lambda i, j: (conv_layer, 0, j + nj)),
            pl.BlockSpec((None, 1, tn), lambda i, j: (conv_layer, 0, j)),
            pl.BlockSpec((None, 1, tn), lambda i, j: (conv_layer, 0, j + nj)),
        ],
        out_specs=pl.BlockSpec((tm, tn), lambda i, j: (i, j)),
        out_shape=jax.ShapeDtypeStruct((t, d), F32),
        scratch_shapes=[pltpu.VMEM((tm, d), BF16)],
        compiler_params=_params("parallel", "arbitrary"),
        name="conv_pw1_glu",
    )(x, norm_g, w, w, b, b)


def _conv_kernel(tiles_per_seq, u_ref, halo_ref, dww_ref, dwb_ref, lng_ref, lnb_ref, w2_ref,
                 b2_ref, x_ref, o_ref, ext_ref, cv_ref, y_ref, w2b_ref):
    ts, d = u_ref.shape
    n_slabs = d // LANES
    pad = CONV_HALO - (CONV_WIDTH - 1)
    s = pl.program_id(0)
    conv_tile = jnp.minimum(s, pl.num_programs(0) - 2)
    first = lax.rem(conv_tile, tiles_per_seq) == 0

    @pl.when(s == 0)
    def _():
        y_ref[...] = jnp.zeros(y_ref.shape, BF16)
        w2b_ref[...] = w2_ref[...].astype(BF16)

    for c in range(n_slabs):
        lanes = slice(c * LANES, (c + 1) * LANES)
        halo = halo_ref[:, lanes]
        ext_ref[c, 0:CONV_HALO, :] = jnp.where(first, 0.0, halo)
        ext_ref[c, CONV_HALO:, :] = u_ref[:, lanes]

    slabs_per_piece = CONV_PW2_COLS // LANES
    for piece in range(d // CONV_PW2_COLS):
        for c in range(piece * slabs_per_piece, (piece + 1) * slabs_per_piece):
            row_starts = range(0, ts, CONV_ROWS)
            accs = [jnp.broadcast_to(dwb_ref[c], (CONV_ROWS, LANES)) for _ in row_starts]
            for k in range(CONV_WIDTH):
                w_row = dww_ref[c, k:k + 1, :]
                for a, r0 in enumerate(row_starts):
                    accs[a] = accs[a] + w_row * ext_ref[c, r0 + pad + k:r0 + pad + k + CONV_ROWS, :]
            for a, r0 in enumerate(row_starts):
                cv_ref[c, r0:r0 + CONV_ROWS, :] = accs[a]
        cols = slice(piece * CONV_PW2_COLS, (piece + 1) * CONV_PW2_COLS)
        o_ref[:, cols] = (x_ref[:, cols] + b2_ref[:, cols]
                          + jnp.dot(y_ref[...], w2b_ref[:, cols], preferred_element_type=F32))

    total = cv_ref[0]
    for c in range(1, n_slabs):
        total = total + cv_ref[c]
    mu = jnp.sum(total, axis=-1, keepdims=True) * (1.0 / d)
    sq = jnp.zeros((ts, LANES), F32)
    for c in range(n_slabs):
        xc = cv_ref[c] - mu
        sq = sq + xc * xc
    rstd = lax.rsqrt(jnp.sum(sq, axis=-1, keepdims=True) * (1.0 / d) + NORM_EPS)
    for c in range(n_slabs):
        lanes = slice(c * LANES, (c + 1) * LANES)
        y = (cv_ref[c] - mu) * rstd * lng_ref[:, lanes] + lnb_ref[:, lanes]
        y_ref[:, lanes] = (y * _sigmoid(y)).astype(BF16)


def _conv_module(u, x, dw_w, dw_b, ln_g, ln_b, w2, b2, conv_layer, batch):
    t, d = x.shape
    ts = CONV_TS
    n_slabs = d // LANES
    n_tiles = t // ts
    tiles_per_seq = t // batch // ts
    hpt = ts // CONV_HALO

    conv_row = lambda s: (jnp.minimum(s, n_tiles - 1), 0)
    halo_row = lambda s: (jnp.maximum(jnp.minimum(s, n_tiles - 1) * hpt - 1, 0), 0)
    pw2_row = lambda s: (jnp.maximum(s - 1, 0), 0)
    vec = lambda s: (conv_layer, 0, 0)
    return pl.pallas_call(
        functools.partial(_conv_kernel, tiles_per_seq),
        grid=(n_tiles + 1,),
        in_specs=[
            pl.BlockSpec((ts, d), conv_row),
            pl.BlockSpec((CONV_HALO, d), halo_row),
            pl.BlockSpec((None, n_slabs, CONV_WIDTH, LANES), lambda s: (conv_layer, 0, 0, 0)),
            pl.BlockSpec((None, n_slabs, 1, LANES), lambda s: (conv_layer, 0, 0, 0)),
            pl.BlockSpec((None, 1, d), vec),
            pl.BlockSpec((None, 1, d), vec),
            pl.BlockSpec((None, d, d), vec, pipeline_mode=pl.Buffered(1)),
            pl.BlockSpec((None, 1, d), vec),
            pl.BlockSpec((ts, d), pw2_row),
        ],
        out_specs=pl.BlockSpec((ts, d), pw2_row),
        out_shape=jax.ShapeDtypeStruct((t, d), F32),
        scratch_shapes=[
            pltpu.VMEM((n_slabs, ts + CONV_HALO, LANES), F32),
            pltpu.VMEM((n_slabs, ts, LANES), F32),
            pltpu.VMEM((ts, d), BF16),
            pltpu.VMEM((d, d), BF16),
        ],
        compiler_params=_params("arbitrary"),
        name="conv_dw_ln_pw2",
    )(u, u, dw_w, dw_b, ln_g, ln_b, w2, b2, x)


def _qkv_kernel(x_ref, g_ref, w_ref, qn_ref, kn_ref, o_ref, h_ref):
    j = pl.program_id(1)
    heads_per_step = o_ref.shape[0]
    n_q_steps = N_HEADS // heads_per_step

    @pl.when(j == 0)
    def _():
        h_ref[...] = _rms_norm_rows(x_ref[...], g_ref[...]).astype(BF16)

    y = jnp.dot(h_ref[...], w_ref[...].astype(BF16), preferred_element_type=F32)
    is_q = j < n_q_steps
    is_v = j >= 2 * n_q_steps
    gain = jnp.where(is_q, qn_ref[...] * (HEAD_DIM ** -0.5 * LOG2E), kn_ref[...])
    for c in range(heads_per_step):
        t = y[:, c * HEAD_DIM:(c + 1) * HEAD_DIM]
        normed = _rms_norm_rows(t, gain)
        o_ref[c] = jnp.where(is_v, t, normed).astype(BF16)


def _qkv_proj(x, norm_g, w, q_norm, k_norm, layer, attn_layer):
    t, d = x.shape
    tm, tn = PROJ_TM, QKV_TN
    hps = tn // HEAD_DIM
    return pl.pallas_call(
        _qkv_kernel,
        grid=(t // tm, 3 * d // tn),
        in_specs=[
            pl.BlockSpec((tm, d), lambda i, j: (i, 0)),
            pl.BlockSpec((None, 1, d), lambda i, j: (layer, 0, 0)),
            pl.BlockSpec((None, d, tn), lambda i, j: (attn_layer, 0, j)),
            pl.BlockSpec((None, 1, HEAD_DIM), lambda i, j: (attn_layer, 0, 0)),
            pl.BlockSpec((None, 1, HEAD_DIM), lambda i, j: (attn_layer, 0, 0)),
        ],
        out_specs=pl.BlockSpec((hps, tm, HEAD_DIM), lambda i, j: (j, i, 0)),
        out_shape=jax.ShapeDtypeStruct((3 * N_HEADS, t, HEAD_DIM), BF16),
        scratch_shapes=[pltpu.VMEM((tm, d), BF16)],
        compiler_params=_params("parallel", "arbitrary"),
        name="attn_qkv",
    )(x, norm_g, w, q_norm, k_norm)


def _t5_bucket(dist):
    max_exact = NUM_BUCKETS // 2
    n = jnp.maximum(dist, 0)
    nf = jnp.maximum(n, 1).astype(F32)
    large = max_exact + (jnp.log(nf / max_exact) / math.log(MAX_DISTANCE / max_exact)
                         * (NUM_BUCKETS - max_exact)).astype(jnp.int32)
    large = jnp.minimum(large, NUM_BUCKETS - 1)
    return jnp.where(n < max_exact, n, large)


def _bias_tile_cap(n_blocks):
    max_exact = NUM_BUCKETS // 2
    d = max_exact
    while max_exact + int(math.log(d / max_exact) / math.log(MAX_DISTANCE / max_exact)
                          * (NUM_BUCKETS - max_exact)) < NUM_BUCKETS - 1:
        d += 1
    cap = -(-(d + MOBA_BLOCK - 1) // MOBA_BLOCK)
    return min(cap, n_blocks - 1)


def _attn_kernel(tbl_ref, bucket_ref, q_ref, k_ref, v_ref, o_ref,
                 bias_ref, pen_ref, vt_ref, s0_ref, s1_ref, smax0_ref, smax1_ref,
                 m_ref, acc_ref):
    hg = pl.program_id(0)
    b = pl.program_id(1)
    i = pl.program_id(2)
    blk = MOBA_BLOCK
    n_heads, s_len, hd = q_ref.shape
    nb = s_len // blk
    chunk = ATTN_CHUNK_BLOCKS
    cw = chunk * blk
    n_tiles = bias_ref.shape[1]
    cap = n_tiles - (2 * chunk - 1)
    nt_dims = (((1,), (1,)), ((), ()))

    @pl.when((b == 0) & (i == 0))
    def _():
        x_idx = lax.broadcasted_iota(jnp.int32, (blk, blk), 0)
        y_idx = lax.broadcasted_iota(jnp.int32, (blk, blk), 1)
        for g in range(n_heads):
            h = hg * n_heads + g
            for j in range(n_tiles):
                dl = min(cap + chunk - 1 - j, cap)
                if dl < 0:
                    bias_ref[g, j] = jnp.full((blk, blk), NEG_INF, F32)
                    continue
                bk = bucket_ref[j:j + 1, :]
                w = jnp.zeros(bk.shape, F32)
                for bucket in range(NUM_BUCKETS):
                    w = jnp.where(bk == bucket, tbl_ref[h, bucket], w)
                rows = jnp.broadcast_to(w * LOG2E, (blk, 2 * blk))
                tile = pltpu.roll(rows, blk + 1, 1, stride=1, stride_axis=0)[:, :blk]
                if dl == 0:
                    tile = jnp.where(x_idx <= y_idx, tile, NEG_INF)
                bias_ref[g, j] = tile

    @pl.when(i == 0)
    def _():
        for g in range(n_heads):
            q = q_ref[g]
            k = k_ref[g]
            kmean = jnp.sum(k.astype(F32).reshape(nb, blk, hd), axis=1) * (1.0 / blk)
            km_hi = kmean.astype(BF16)
            km_lo = (kmean - km_hi.astype(F32)).astype(BF16)
            gate = (lax.dot_general(km_hi, q, nt_dims, preferred_element_type=F32)
                    + lax.dot_general(km_lo, q, nt_dims, preferred_element_type=F32))
            n_idx = lax.broadcasted_iota(jnp.int32, gate.shape, 0)
            own = lax.shift_right_logical(
                lax.broadcasted_iota(jnp.int32, gate.shape, 1), int(math.log2(blk)))
            cnt = jnp.zeros(gate.shape, jnp.int32)
            for n in range(nb):
                row = gate[n:n + 1, :]
                beats = jnp.where(row > gate, 1, jnp.where((row == gate) & (n < n_idx), 1, 0))
                cnt = cnt + jnp.where(n < own, beats, 0)
            keep = ((n_idx < own) & (cnt < MOBA_TOPK)) | (n_idx == own)
            pen = jnp.where(keep, 0.0, NEG_INF)
            for t in range(s_len // cw):
                pen_ref[g, t] = pen[:, t * cw:(t + 1) * cw]
            v = v_ref[g].astype(F32)
            ones_row = (lax.broadcasted_iota(jnp.int32, (ATTN_SUM_ROWS, cw), 0) == 0)
            for c in range(s_len // cw):
                vt_ref[g, c, :hd, :] = v[c * cw:(c + 1) * cw, :].T.astype(BF16)
                vt_ref[g, c, hd:, :] = jnp.where(ones_row, 1.0, 0.0).astype(BF16)

    q0 = pl.multiple_of(i * cw, cw)

    def pen_rows(g, c):
        return [pen_ref[g, i, pl.ds(c * chunk + kb, 1), :] for kb in range(chunk)]

    def logits(g, c, slot):
        s_ref, smax_ref = slots[slot]
        k0 = pl.multiple_of(c * cw, cw)
        s = lax.dot_general(k_ref[g, pl.ds(k0, cw), :], q_ref[g, pl.ds(q0, cw), :],
                            nt_dims, preferred_element_type=F32)
        pen = pen_rows(g, c)
        for qb in range(chunk):
            cols = slice(qb * blk, (qb + 1) * blk)
            j0 = jnp.maximum(cap + (chunk - 1) - ((i - c) * chunk + qb), 0)
            sq = s[:, cols] + bias_ref[g, pl.ds(j0, chunk)].reshape(cw, blk)
            s_ref[g, :, cols] = sq
            block_max = [jnp.max(sq[kb * blk:(kb + 1) * blk, :], axis=0, keepdims=True)
                         + pen[kb][:, cols] for kb in range(chunk)]
            smax_ref[g, :, cols] = functools.reduce(jnp.maximum, block_max)

    def softmax_step(g, c, slot):
        s_ref, smax_ref = slots[slot]
        m = m_ref[g]
        m_new = jnp.maximum(m, smax_ref[g])
        alpha = jnp.exp2(m - m_new)
        p = jnp.concatenate(
            [jnp.exp2(s_ref[g, kb * blk:(kb + 1) * blk, :] - jnp.where(pen_kb < 0.0, -NEG_INF, m_new))
             for kb, pen_kb in enumerate(pen_rows(g, c))], axis=0)
        m_ref[g] = m_new
        acc_ref[g] = alpha * acc_ref[g] + jnp.dot(vt_ref[g, c], p.astype(BF16),
                                                   preferred_element_type=F32)

    def step(c, slot):
        for g in range(n_heads):
            logits(g, c + 1, 1 - slot)
            softmax_step(g, c, slot)

    slots = ((s0_ref, smax0_ref), (s1_ref, smax1_ref))
    odd = i % 2
    for g in range(n_heads):
        m_ref[g] = jnp.full((1, cw), NEG_INF, F32)
        acc_ref[g] = jnp.zeros(acc_ref.shape[1:], F32)

    @pl.when(odd == 0)
    def _():
        for g in range(n_heads):
            logits(g, 0, 0)

    @pl.when(odd == 1)
    def _():
        for g in range(n_heads):
            logits(g, 0, 1)
        step(0, 1)

    def pair(j, carry):
        c = odd + 2 * j
        step(c, 0)
        step(c + 1, 1)
        return carry

    lax.fori_loop(0, i // 2, pair, 0)
    for g in range(n_heads):
        softmax_step(g, i, 0)
        acc = acc_ref[g]
        out = acc[:hd, :] * (1.0 / acc[hd:hd + 1, :])
        o_ref[:, g * hd:(g + 1) * hd] = out.T.astype(BF16)


def _moba_attention(qkv, rel_bias, batch):
    _, t, hd = qkv.shape
    s = t // batch
    blk = MOBA_BLOCK
    nb = s // blk
    g, chunk = ATTN_HEADS_PER_STEP, ATTN_CHUNK_BLOCKS
    groups = N_HEADS // g
    cap = _bias_tile_cap(nb)
    n_tiles = cap + 2 * chunk - 1
    dl = jnp.minimum(cap + (chunk - 1) - jnp.arange(n_tiles, dtype=jnp.int32), cap)[:, None]
    n = jnp.arange(2 * blk, dtype=jnp.int32)[None, :]
    buckets = _t5_bucket(dl * blk + n - (blk - 1))
    return pl.pallas_call(
        _attn_kernel,
        grid=(groups, batch, nb // chunk),
        in_specs=[
            pl.BlockSpec(memory_space=pltpu.SMEM),
            pl.BlockSpec((n_tiles, 2 * blk), lambda hg, b, i: (0, 0)),
            pl.BlockSpec((g, s, hd), lambda hg, b, i: (hg, b, 0)),
            pl.BlockSpec((g, s, hd), lambda hg, b, i: (groups + hg, b, 0)),
            pl.BlockSpec((g, s, hd), lambda hg, b, i: (2 * groups + hg, b, 0)),
        ],
        out_specs=pl.BlockSpec((chunk * blk, g * hd), lambda hg, b, i: (b * (nb // chunk) + i, hg)),
        out_shape=jax.ShapeDtypeStruct((t, N_HEADS * hd), BF16),
        scratch_shapes=[
            pltpu.VMEM((g, n_tiles, blk, blk), F32),
            pltpu.VMEM((g, s // (chunk * blk), nb, chunk * blk), F32),
            pltpu.VMEM((g, s // (chunk * blk), hd + ATTN_SUM_ROWS, chunk * blk), BF16),
            pltpu.VMEM((g, chunk * blk, chunk * blk), F32),
            pltpu.VMEM((g, chunk * blk, chunk * blk), F32),
            pltpu.VMEM((g, 1, chunk * blk), F32),
            pltpu.VMEM((g, 1, chunk * blk), F32),
            pltpu.VMEM((g, 1, chunk * blk), F32),
            pltpu.VMEM((g, hd + ATTN_SUM_ROWS, chunk * blk), F32),
        ],
        compiler_params=_params("arbitrary", "arbitrary", "arbitrary"),
        name="moba_attention",
    )(rel_bias.T.astype(F32), buckets, qkv, qkv, qkv)


def _wo_kernel(a_ref, w_ref, x_ref, o_ref, wb_ref):
    @pl.when(pl.program_id(0) == 0)
    def _():
        wb_ref[...] = w_ref[...].astype(BF16)

    o_ref[...] = x_ref[...] + jnp.dot(a_ref[...], wb_ref[...], preferred_element_type=F32)


def _wo_proj(a, x, w, attn_layer):
    t, d = x.shape
    tm = OUT_TM
    return pl.pallas_call(
        _wo_kernel,
        grid=(t // tm,),
        in_specs=[
            pl.BlockSpec((tm, d), lambda i: (i, 0)),
            pl.BlockSpec((None, d, d), lambda i: (attn_layer, 0, 0), pipeline_mode=pl.Buffered(1)),
            pl.BlockSpec((tm, d), lambda i: (i, 0)),
        ],
        out_specs=pl.BlockSpec((tm, d), lambda i: (i, 0)),
        out_shape=jax.ShapeDtypeStruct((t, d), F32),
        scratch_shapes=[pltpu.VMEM((d, d), BF16)],
        compiler_params=_params("arbitrary"),
        name="attn_wo",
    )(a, w, x)


def kernel(x, rel_bias, ffn1_norm, ffn1_w_gate, ffn1_w_up, ffn1_w_down, mix_norm, ffn2_norm, ffn2_w_gate, ffn2_w_up, ffn2_w_down, conv_pw1_w, conv_pw1_b, conv_dw_w, conv_dw_b, conv_ln_g, conv_ln_b, conv_pw2_w, conv_pw2_b, attn_wqkv, attn_q_norm, attn_k_norm, attn_wo):
    batch, seq, d = x.shape
    depth = ffn1_norm.shape[0]
    n_slabs = d // LANES
    vec = lambda v: v.reshape(v.shape[0], 1, v.shape[-1])
    ffn1 = (vec(ffn1_norm), ffn1_w_gate, ffn1_w_up, ffn1_w_down)
    ffn2 = (vec(ffn2_norm), ffn2_w_gate, ffn2_w_up, ffn2_w_down)
    mix_g = vec(mix_norm)
    pw1_w, pw1_b = conv_pw1_w, vec(conv_pw1_b)
    nc = conv_dw_w.shape[0]
    dw_w = conv_dw_w.reshape(nc, CONV_WIDTH, n_slabs, LANES).transpose(0, 2, 1, 3)
    dw_b = conv_dw_b.reshape(nc, n_slabs, 1, LANES)
    pw2_w = conv_pw2_w
    wqkv, wo = attn_wqkv, attn_wo

    xt = x.reshape(batch * seq, d)
    for i in range(depth):
        xt = _ffn(xt, *ffn1, i)
        j = i // N_MIXERS
        if i % N_MIXERS == 0:
            u = _pw1_glu(xt, mix_g, pw1_w, pw1_b, i, j)
            xt = _conv_module(u, xt, dw_w, dw_b, vec(conv_ln_g), vec(conv_ln_b),
                              pw2_w, vec(conv_pw2_b), j, batch)
        else:
            qkv = _qkv_proj(xt, mix_g, wqkv, vec(attn_q_norm), vec(attn_k_norm), i, j)
            a = _moba_attention(qkv, rel_bias, batch)
            xt = _wo_proj(a, xt, wo, j)
        xt = _ffn(xt, *ffn2, i)
    return xt.reshape(batch, seq, d)
```

```python
import functools
import math

import jax
import jax.numpy as jnp
from jax import lax
from jax.experimental import pallas as pl
from jax.experimental.pallas import tpu as pltpu

D_MODEL = 2048
N_HEADS = 16
HEAD_DIM = D_MODEL // N_HEADS
D_FF = 5632
CONV_WIDTH = 31
MOBA_BLOCK = 256
MOBA_TOPK = 3
NUM_BUCKETS = 32
MAX_DISTANCE = 2048
NORM_EPS = 1e-6
NEG_INF = -1e30
N_MIXERS = 2

LANES = 128
SUBLANES = 8
VMEM_LIMIT_BYTES = 60000 * 1024

BF16 = jnp.bfloat16
F32 = jnp.float32

FFN_TM = 1024
FFN_TF = 512
FFN_REFILL_STEP = 1
PROJ_TM = 1024
PROJ_TN = 512
QKV_TN = 1024
OUT_TM = 512
CONV_TS = 256
CONV_HALO = 32
CONV_ROWS = 128
CONV_PW2_COLS = 256
ATTN_HEADS_PER_STEP = 4
ATTN_CHUNK_BLOCKS = 2
ATTN_SUM_ROWS = 16
LOG2E = math.log2(math.e)


def _params(*semantics):
    return pltpu.CompilerParams(
        dimension_semantics=semantics, vmem_limit_bytes=VMEM_LIMIT_BYTES)


def _rms_norm_rows(x, g):
    ms = jnp.mean(x * x, axis=-1, keepdims=True)
    return (x * lax.rsqrt(ms + NORM_EPS)) * g


def _sigmoid(x):
    return 1.0 / (1.0 + jnp.exp(-x))


def _ffn_kernel(x_hbm, g_ref, wg_ref, wu_ref, wd_ref, o_hbm, acc_ref, h_ref, x_sem, o_sem):
    i = pl.program_id(0)
    f = pl.program_id(1)
    n_i = pl.num_programs(0)
    n_f = pl.num_programs(1)
    tm = acc_ref.shape[1]
    cur = lax.rem(i, 2)
    oth = 1 - cur

    def tile_rows(tile):
        return pl.ds(pl.multiple_of(tile * tm, tm), tm)

    def x_copy(tile, slot):
        return pltpu.make_async_copy(x_hbm.at[tile_rows(tile), :], acc_ref.at[slot], x_sem)

    def o_copy(tile, slot):
        return pltpu.make_async_copy(acc_ref.at[slot], o_hbm.at[tile_rows(tile), :], o_sem)

    def norm_rows(src_slot, dst_slot, r0, n_rows):
        x = acc_ref[src_slot, pl.ds(r0, n_rows), :]
        h_ref[dst_slot, pl.ds(r0, n_rows), :] = _rms_norm_rows(x, g_ref[...]).astype(BF16)

    @pl.when((i == 0) & (f == 0))
    def _():
        x_copy(0, 0).start()
        x_copy(0, 0).wait()
        norm_rows(0, 0, 0, tm)

    @pl.when((f == FFN_REFILL_STEP) & (i > 0))
    def _():
        o_copy(i - 1, oth).wait()

    @pl.when((f == FFN_REFILL_STEP) & (i + 1 < n_i))
    def _():
        x_copy(i + 1, oth).start()

    @pl.when((f == FFN_REFILL_STEP + 1) & (i + 1 < n_i))
    def _():
        x_copy(i + 1, oth).wait()

    norm_step0 = FFN_REFILL_STEP + 2
    chunk_rows = tm // (n_f - norm_step0)

    h = h_ref[cur]
    gate = jnp.dot(h, wg_ref[...].astype(BF16), preferred_element_type=F32)
    up = jnp.dot(h, wu_ref[...].astype(BF16), preferred_element_type=F32)

    chunk = jnp.maximum(f - norm_step0, 0)
    src = jnp.where(f < norm_step0, cur, oth)
    norm_rows(src, oth, pl.multiple_of(chunk * chunk_rows, chunk_rows), chunk_rows)

    act = (gate * _sigmoid(gate)) * up * 0.5
    acc_ref[cur] += jnp.dot(act.astype(BF16), wd_ref[...].astype(BF16),
                            preferred_element_type=F32)

    @pl.when(f == n_f - 1)
    def _():
        o_copy(i, cur).start()

    @pl.when((f == n_f - 1) & (i == n_i - 1))
    def _():
        o_copy(i, cur).wait()


def _ffn(x, norm_g, w_gate, w_up, w_down, layer):
    t, d = x.shape
    f = w_gate.shape[-1]
    tm, tf = FFN_TM, FFN_TF
    n_f = f // tf
    n_norm_steps = n_f - (FFN_REFILL_STEP + 2)
    assert n_norm_steps > 0 and tm % n_norm_steps == 0 and (tm // n_norm_steps) % 16 == 0
    return pl.pallas_call(
        _ffn_kernel,
        grid=(t // tm, n_f),
        in_specs=[
            pl.BlockSpec(memory_space=pl.ANY),
            pl.BlockSpec((None, 1, d), lambda i, j: (layer, 0, 0)),
            pl.BlockSpec((None, d, tf), lambda i, j: (layer, 0, j)),
            pl.BlockSpec((None, d, tf), lambda i, j: (layer, 0, j)),
            pl.BlockSpec((None, tf, d), lambda i, j: (layer, j, 0)),
        ],
        out_specs=pl.BlockSpec(memory_space=pl.ANY),
        out_shape=jax.ShapeDtypeStruct((t, d), F32),
        scratch_shapes=[
            pltpu.VMEM((2, tm, d), F32),
            pltpu.VMEM((2, tm, d), BF16),
            pltpu.SemaphoreType.DMA(()),
            pltpu.SemaphoreType.DMA(()),
        ],
        compiler_params=_params("arbitrary", "arbitrary"),
        name="ffn",
    )(x, norm_g, w_gate, w_up, w_down)


def _pw1_kernel(x_ref, g_ref, wa_ref, wg_ref, ba_ref, bg_ref, o_ref, h_ref):
    @pl.when(pl.program_id(1) == 0)
    def _():
        h_ref[...] = _rms_norm_rows(x_ref[...], g_ref[...]).astype(BF16)

    h = h_ref[...]
    a = jnp.dot(h, wa_ref[...].astype(BF16), preferred_element_type=F32) + ba_ref[...]
    g = jnp.dot(h, wg_ref[...].astype(BF16), preferred_element_type=F32) + bg_ref[...]
    o_ref[...] = a * _sigmoid(g)


def _pw1_glu(x, norm_g, w, b, layer, conv_layer):
    t, d = x.shape
    tm, tn = PROJ_TM, PROJ_TN
    nj = d // tn
    return pl.pallas_call(
        _pw1_kernel,
        grid=(t // tm, nj),
        in_specs=[
            pl.BlockSpec((tm, d), lambda i, j: (i, 0)),
            pl.BlockSpec((None, 1, d), lambda i, j: (layer, 0, 0)),
            pl.BlockSpec((None, d, tn), lambda i, j: (conv_layer, 0, j)),
            pl.BlockSpec((None, d, tn), lambda i, j: (conv_layer, 0, j + nj)),
            pl.BlockSpec((None, 1, tn), lambda i, j: (conv_layer, 0, j)),
            pl.BlockSpec((None, 1, tn), lambda i, j: (conv_layer, 0, j + nj)),
        ],
        out_specs=pl.BlockSpec((tm, tn), lambda i, j: (i, j)),
        out_shape=jax.ShapeDtypeStruct((t, d), F32),
        scratch_shapes=[pltpu.VMEM((tm, d), BF16)],
        compiler_params=_params("parallel", "arbitrary"),
        name="conv_pw1_glu",
    )(x, norm_g, w, w, b, b)


def _conv_kernel(tiles_per_seq, u_ref, halo_ref, dww_ref, dwb_ref, lng_ref, lnb_ref, w2_ref,
                 b2_ref, x_ref, o_ref, ext_ref, cv_ref, y_ref, w2b_ref):
    ts, d = u_ref.shape
    n_slabs = d // LANES
    pad = CONV_HALO - (CONV_WIDTH - 1)
    s = pl.program_id(0)
    conv_tile = jnp.minimum(s, pl.num_programs(0) - 2)
    first = lax.rem(conv_tile, tiles_per_seq) == 0

    @pl.when(s == 0)
    def _():
        y_ref[...] = jnp.zeros(y_ref.shape, BF16)
        w2b_ref[...] = w2_ref[...].astype(BF16)

    for c in range(n_slabs):
        lanes = slice(c * LANES, (c + 1) * LANES)
        halo = halo_ref[:, lanes]
        ext_ref[c, 0:CONV_HALO, :] = jnp.where(first, 0.0, halo)
        ext_ref[c, CONV_HALO:, :] = u_ref[:, lanes]

    slabs_per_piece = CONV_PW2_COLS // LANES
    for piece in range(d // CONV_PW2_COLS):
        for c in range(piece * slabs_per_piece, (piece + 1) * slabs_per_piece):
            row_starts = range(0, ts, CONV_ROWS)
            accs = [jnp.broadcast_to(dwb_ref[c], (CONV_ROWS, LANES)) for _ in row_starts]
            for k in range(CONV_WIDTH):
                w_row = dww_ref[c, k:k + 1, :]
                for a, r0 in enumerate(row_starts):
                    accs[a] = accs[a] + w_row * ext_ref[c, r0 + pad + k:r0 + pad + k + CONV_ROWS, :]
            for a, r0 in enumerate(row_starts):
                cv_ref[c, r0:r0 + CONV_ROWS, :] = accs[a]
        cols = slice(piece * CONV_PW2_COLS, (piece + 1) * CONV_PW2_COLS)
        o_ref[:, cols] = (x_ref[:, cols] + b2_ref[:, cols]
                          + jnp.dot(y_ref[...], w2b_ref[:, cols], preferred_element_type=F32))

    total = cv_ref[0]
    for c in range(1, n_slabs):
        total = total + cv_ref[c]
    mu = jnp.sum(total, axis=-1, keepdims=True) * (1.0 / d)
    sq = jnp.zeros((ts, LANES), F32)
    for c in range(n_slabs):
        xc = cv_ref[c] - mu
        sq = sq + xc * xc
    rstd = lax.rsqrt(jnp.sum(sq, axis=-1, keepdims=True) * (1.0 / d) + NORM_EPS)
    for c in range(n_slabs):
        lanes = slice(c * LANES, (c + 1) * LANES)
        y = (cv_ref[c] - mu) * rstd * lng_ref[:, lanes] + lnb_ref[:, lanes]
        y_ref[:, lanes] = (y * _sigmoid(y)).astype(BF16)


def _conv_module(u, x, dw_w, dw_b, ln_g, ln_b, w2, b2, conv_layer, batch):
    t, d = x.shape
    ts = CONV_TS
    n_slabs = d // LANES
    n_tiles = t // ts
    tiles_per_seq = t // batch // ts
    hpt = ts // CONV_HALO

    conv_row = lambda s: (jnp.minimum(s, n_tiles - 1), 0)
    halo_row = lambda s: (jnp.maximum(jnp.minimum(s, n_tiles - 1) * hpt - 1, 0), 0)
    pw2_row = lambda s: (jnp.maximum(s - 1, 0), 0)
    vec = lambda s: (conv_layer, 0, 0)
    return pl.pallas_call(
        functools.partial(_conv_kernel, tiles_per_seq),
        grid=(n_tiles + 1,),
        in_specs=[
            pl.BlockSpec((ts, d), conv_row),
            pl.BlockSpec((CONV_HALO, d), halo_row),
            pl.BlockSpec((None, n_slabs, CONV_WIDTH, LANES), lambda s: (conv_layer, 0, 0, 0)),
            pl.BlockSpec((None, n_slabs, 1, LANES), lambda s: (conv_layer, 0, 0, 0)),
            pl.BlockSpec((None, 1, d), vec),
            pl.BlockSpec((None, 1, d), vec),
            pl.BlockSpec((None, d, d), vec, pipeline_mode=pl.Buffered(1)),
            pl.BlockSpec((None, 1, d), vec),
            pl.BlockSpec((ts, d), pw2_row),
        ],
        out_specs=pl.BlockSpec((ts, d), pw2_row),
        out_shape=jax.ShapeDtypeStruct((t, d), F32),
        scratch_shapes=[
            pltpu.VMEM((n_slabs, ts + CONV_HALO, LANES), F32),
            pltpu.VMEM((n_slabs, ts, LANES), F32),
            pltpu.VMEM((ts, d), BF16),
            pltpu.VMEM((d, d), BF16),
        ],
        compiler_params=_params("arbitrary"),
        name="conv_dw_ln_pw2",
    )(u, u, dw_w, dw_b, ln_g, ln_b, w2, b2, x)


def _qkv_kernel(x_ref, g_ref, w_ref, qn_ref, kn_ref, o_ref, h_ref):
    j = pl.program_id(1)
    heads_per_step = o_ref.shape[0]
    n_q_steps = N_HEADS // heads_per_step

    @pl.when(j == 0)
    def _():
        h_ref[...] = _rms_norm_rows(x_ref[...], g_ref[...]).astype(BF16)

    y = jnp.dot(h_ref[...], w_ref[...].astype(BF16), preferred_element_type=F32)
    is_q = j < n_q_steps
    is_v = j >= 2 * n_q_steps
    gain = jnp.where(is_q, qn_ref[...] * (HEAD_DIM ** -0.5 * LOG2E), kn_ref[...])
    for c in range(heads_per_step):
        t = y[:, c * HEAD_DIM:(c + 1) * HEAD_DIM]
        normed = _rms_norm_rows(t, gain)
        o_ref[c] = jnp.where(is_v, t, normed).astype(BF16)


def _qkv_proj(x, norm_g, w, q_norm, k_norm, layer, attn_layer):
    t, d = x.shape
    tm, tn = PROJ_TM, QKV_TN
    hps = tn // HEAD_DIM
    return pl.pallas_call(
        _qkv_kernel,
        grid=(t // tm, 3 * d // tn),
        in_specs=[
            pl.BlockSpec((tm, d), lambda i, j: (i, 0)),
            pl.BlockSpec((None, 1, d), lambda i, j: (layer, 0, 0)),
            pl.BlockSpec((None, d, tn), lambda i, j: (attn_layer, 0, j)),
            pl.BlockSpec((None, 1, HEAD_DIM), lambda i, j: (attn_layer, 0, 0)),
            pl.BlockSpec((None, 1, HEAD_DIM), lambda i, j: (attn_layer, 0, 0)),
        ],
        out_specs=pl.BlockSpec((hps, tm, HEAD_DIM), lambda i, j: (j, i, 0)),
        out_shape=jax.ShapeDtypeStruct((3 * N_HEADS, t, HEAD_DIM), BF16),
        scratch_shapes=[pltpu.VMEM((tm, d), BF16)],
        compiler_params=_params("parallel", "arbitrary"),
        name="attn_qkv",
    )(x, norm_g, w, q_norm, k_norm)


def _t5_bucket(dist):
    max_exact = NUM_BUCKETS // 2
    n = jnp.maximum(dist, 0)
    nf = jnp.maximum(n, 1).astype(F32)
    large = max_exact + (jnp.log(nf / max_exact) / math.log(MAX_DISTANCE / max_exact)
                         * (NUM_BUCKETS - max_exact)).astype(jnp.int32)
    large = jnp.minimum(large, NUM_BUCKETS - 1)
    return jnp.where(n < max_exact, n, large)


def _bias_tile_cap(n_blocks):
    max_exact = NUM_BUCKETS // 2
    d = max_exact
    while max_exact + int(math.log(d / max_exact) / math.log(MAX_DISTANCE / max_exact)
                          * (NUM_BUCKETS - max_exact)) < NUM_BUCKETS - 1:
        d += 1
    cap = -(-(d + MOBA_BLOCK - 1) // MOBA_BLOCK)
    return min(cap, n_blocks - 1)


def _attn_kernel(tbl_ref, bucket_ref, q_ref, k_ref, v_ref, o_ref,
                 bias_ref, pen_ref, vt_ref, s0_ref, s1_ref, smax0_ref, smax1_ref,
                 m_ref, acc_ref):
    hg = pl.program_id(0)
    b = pl.program_id(1)
    i = pl.program_id(2)
    blk = MOBA_BLOCK
    n_heads, s_len, hd = q_ref.shape
    nb = s_len // blk
    chunk = ATTN_CHUNK_BLOCKS
    cw = chunk * blk
    n_tiles = bias_ref.shape[1]
    cap = n_tiles - (2 * chunk - 1)
    nt_dims = (((1,), (1,)), ((), ()))

    @pl.when((b == 0) & (i == 0))
    def _():
        x_idx = lax.broadcasted_iota(jnp.int32, (blk, blk), 0)
        y_idx = lax.broadcasted_iota(jnp.int32, (blk, blk), 1)
        for g in range(n_heads):
            h = hg * n_heads + g
            for j in range(n_tiles):
                dl = min(cap + chunk - 1 - j, cap)
                if dl < 0:
                    bias_ref[g, j] = jnp.full((blk, blk), NEG_INF, F32)
                    continue
                bk = bucket_ref[j:j + 1, :]
                w = jnp.zeros(bk.shape, F32)
                for bucket in range(NUM_BUCKETS):
                    w = jnp.where(bk == bucket, tbl_ref[h, bucket], w)
                rows = jnp.broadcast_to(w * LOG2E, (blk, 2 * blk))
                tile = pltpu.roll(rows, blk + 1, 1, stride=1, stride_axis=0)[:, :blk]
                if dl == 0:
                    tile = jnp.where(x_idx <= y_idx, tile, NEG_INF)
                bias_ref[g, j] = tile

    @pl.when(i == 0)
    def _():
        for g in range(n_heads):
            q = q_ref[g]
            k = k_ref[g]
            kmean = jnp.sum(k.astype(F32).reshape(nb, blk, hd), axis=1) * (1.0 / blk)
            km_hi = kmean.astype(BF16)
            km_lo = (kmean - km_hi.astype(F32)).astype(BF16)
            gate = (lax.dot_general(km_hi, q, nt_dims, preferred_element_type=F32)
                    + lax.dot_general(km_lo, q, nt_dims, preferred_element_type=F32))
            n_idx = lax.broadcasted_iota(jnp.int32, gate.shape, 0)
            own = lax.shift_right_logical(
                lax.broadcasted_iota(jnp.int32, gate.shape, 1), int(math.log2(blk)))
            cnt = jnp.zeros(gate.shape, jnp.int32)
            for n in range(nb):
                row = gate[n:n + 1, :]
                beats = jnp.where(row > gate, 1, jnp.where((row == gate) & (n < n_idx), 1, 0))
                cnt = cnt + jnp.where(n < own, beats, 0)
            keep = ((n_idx < own) & (cnt < MOBA_TOPK)) | (n_idx == own)
            pen = jnp.where(keep, 0.0, NEG_INF)
            for t in range(s_len // cw):
                pen_ref[g, t] = pen[:, t * cw:(t + 1) * cw]
            v = v_ref[g].astype(F32)
            ones_row = (lax.broadcasted_iota(jnp.int32, (ATTN_SUM_ROWS, cw), 0) == 0)
            for c in range(s_len // cw):
                vt_ref[g, c, :hd, :] = v[c * cw:(c + 1) * cw, :].T.astype(BF16)
                vt_ref[g, c, hd:, :] = jnp.where(ones_row, 1.0, 0.0).astype(BF16)

    q0 = pl.multiple_of(i * cw, cw)

    def pen_rows(g, c):
        return [pen_ref[g, i, pl.ds(c * chunk + kb, 1), :] for kb in range(chunk)]

    def logits(g, c, slot):
        s_ref, smax_ref = slots[slot]
        k0 = pl.multiple_of(c * cw, cw)
        s = lax.dot_general(k_ref[g, pl.ds(k0, cw), :], q_ref[g, pl.ds(q0, cw), :],
                            nt_dims, preferred_element_type=F32)
        pen = pen_rows(g, c)
        for qb in range(chunk):
            cols = slice(qb * blk, (qb + 1) * blk)
            j0 = jnp.maximum(cap + (chunk - 1) - ((i - c) * chunk + qb), 0)
            sq = s[:, cols] + bias_ref[g, pl.ds(j0, chunk)].reshape(cw, blk)
            s_ref[g, :, cols] = sq
            block_max = [jnp.max(sq[kb * blk:(kb + 1) * blk, :], axis=0, keepdims=True)
                         + pen[kb][:, cols] for kb in range(chunk)]
            smax_ref[g, :, cols] = functools.reduce(jnp.maximum, block_max)

    def softmax_step(g, c, slot):
        s_ref, smax_ref = slots[slot]
        m = m_ref[g]
        m_new = jnp.maximum(m, smax_ref[g])
        alpha = jnp.exp2(m - m_new)
        p = jnp.concatenate(
            [jnp.exp2(s_ref[g, kb * blk:(kb + 1) * blk, :] - jnp.where(pen_kb < 0.0, -NEG_INF, m_new))
             for kb, pen_kb in enumerate(pen_rows(g, c))], axis=0)
        m_ref[g] = m_new
        acc_ref[g] = alpha * acc_ref[g] + jnp.dot(vt_ref[g, c], p.astype(BF16),
                                                   preferred_element_type=F32)

    def step(c, slot):
        for g in range(n_heads):
            logits(g, c + 1, 1 - slot)
            softmax_step(g, c, slot)

    slots = ((s0_ref, smax0_ref), (s1_ref, smax1_ref))
    odd = i % 2
    for g in range(n_heads):
        m_ref[g] = jnp.full((1, cw), NEG_INF, F32)
        acc_ref[g] = jnp.zeros(acc_ref.shape[1:], F32)

    @pl.when(odd == 0)
    def _():
        for g in range(n_heads):
            logits(g, 0, 0)

    @pl.when(odd == 1)
    def _():
        for g in range(n_heads):
            logits(g, 0, 1)
        step(0, 1)

    def pair(j, carry):
        c = odd + 2 * j
        step(c, 0)
        step(c + 1, 1)
        return carry

    lax.fori_loop(0, i // 2, pair, 0)
    for g in range(n_heads):
        softmax_step(g, i, 0)
        acc = acc_ref[g]
        out = acc[:hd, :] * (1.0 / acc[hd:hd + 1, :])
        o_ref[:, g * hd:(g + 1) * hd] = out.T.astype(BF16)


def _moba_attention(qkv, rel_bias, batch):
    _, t, hd = qkv.shape
    s = t // batch
    blk = MOBA_BLOCK
    nb = s // blk
    g, chunk = ATTN_HEADS_PER_STEP, ATTN_CHUNK_BLOCKS
    groups = N_HEADS // g
    cap = _bias_tile_cap(nb)
    n_tiles = cap + 2 * chunk - 1
    dl = jnp.minimum(cap + (chunk - 1) - jnp.arange(n_tiles, dtype=jnp.int32), cap)[:, None]
    n = jnp.arange(2 * blk, dtype=jnp.int32)[None, :]
    buckets = _t5_bucket(dl * blk + n - (blk - 1))
    return pl.pallas_call(
        _attn_kernel,
        grid=(groups, batch, nb // chunk),
        in_specs=[
            pl.BlockSpec(memory_space=pltpu.SMEM),
            pl.BlockSpec((n_tiles, 2 * blk), lambda hg, b, i: (0, 0)),
            pl.BlockSpec((g, s, hd), lambda hg, b, i: (hg, b, 0)),
            pl.BlockSpec((g, s, hd), lambda hg, b, i: (groups + hg, b, 0)),
            pl.BlockSpec((g, s, hd), lambda hg, b, i: (2 * groups + hg, b, 0)),
        ],
        out_specs=pl.BlockSpec((chunk * blk, g * hd), lambda hg, b, i: (b * (nb // chunk) + i, hg)),
        out_shape=jax.ShapeDtypeStruct((t, N_HEADS * hd), BF16),
        scratch_shapes=[
            pltpu.VMEM((g, n_tiles, blk, blk), F32),
            pltpu.VMEM((g, s // (chunk * blk), nb, chunk * blk), F32),
            pltpu.VMEM((g, s // (chunk * blk), hd + ATTN_SUM_ROWS, chunk * blk), BF16),
            pltpu.VMEM((g, chunk * blk, chunk * blk), F32),
            pltpu.VMEM((g, chunk * blk, chunk * blk), F32),
            pltpu.VMEM((g, 1, chunk * blk), F32),
            pltpu.VMEM((g, 1, chunk * blk), F32),
            pltpu.VMEM((g, 1, chunk * blk), F32),
            pltpu.VMEM((g, hd + ATTN_SUM_ROWS, chunk * blk), F32),
        ],
        compiler_params=_params("arbitrary", "arbitrary", "arbitrary"),
        name="moba_attention",
    )(rel_bias.T.astype(F32), buckets, qkv, qkv, qkv)


def _wo_kernel(a_ref, w_ref, x_ref, o_ref, wb_ref):
    @pl.when(pl.program_id(0) == 0)
    def _():
        wb_ref[...] = w_ref[...].astype(BF16)

    o_ref[...] = x_ref[...] + jnp.dot(a_ref[...], wb_ref[...], preferred_element_type=F32)


def _wo_proj(a, x, w, attn_layer):
    t, d = x.shape
    tm = OUT_TM
    return pl.pallas_call(
        _wo_kernel,
        grid=(t // tm,),
        in_specs=[
            pl.BlockSpec((tm, d), lambda i: (i, 0)),
            pl.BlockSpec((None, d, d), lambda i: (attn_layer, 0, 0), pipeline_mode=pl.Buffered(1)),
            pl.BlockSpec((tm, d), lambda i: (i, 0)),
        ],
        out_specs=pl.BlockSpec((tm, d), lambda i: (i, 0)),
        out_shape=jax.ShapeDtypeStruct((t, d), F32),
        scratch_shapes=[pltpu.VMEM((d, d), BF16)],
        compiler_params=_params("arbitrary"),
        name="attn_wo",
    )(a, w, x)


def kernel(x, rel_bias, ffn1_norm, ffn1_w_gate, ffn1_w_up, ffn1_w_down, mix_norm, ffn2_norm, ffn2_w_gate, ffn2_w_up, ffn2_w_down, conv_pw1_w, conv_pw1_b, conv_dw_w, conv_dw_b, conv_ln_g, conv_ln_b, conv_pw2_w, conv_pw2_b, attn_wqkv, attn_q_norm, attn_k_norm, attn_wo):
    batch, seq, d = x.shape
    depth = ffn1_norm.shape[0]
    n_slabs = d // LANES
    vec = lambda v: v.reshape(v.shape[0], 1, v.shape[-1])
    ffn1 = (vec(ffn1_norm), ffn1_w_gate, ffn1_w_up, ffn1_w_down)
    ffn2 = (vec(ffn2_norm), ffn2_w_gate, ffn2_w_up, ffn2_w_down)
    mix_g = vec(mix_norm)
    pw1_w, pw1_b = conv_pw1_w, vec(conv_pw1_b)
    nc = conv_dw_w.shape[0]
    dw_w = conv_dw_w.reshape(nc, CONV_WIDTH, n_slabs, LANES).transpose(0, 2, 1, 3)
    dw_b = conv_dw_b.reshape(nc, n_slabs, 1, LANES)
    pw2_w = conv_pw2_w
    wqkv, wo = attn_wqkv, attn_wo

    xt = x.reshape(batch * seq, d)
    for i in range(depth):
        xt = _ffn(xt, *ffn1, i)
        j = i // N_MIXERS
        if i % N_MIXERS == 0:
            u = _pw1_glu(xt, mix_g, pw1_w, pw1_b, i, j)
            xt = _conv_module(u, xt, dw_w, dw_b, vec(conv_ln_g), vec(conv_ln_b),
                              pw2_w, vec(conv_pw2_b), j, batch)
        else:
            qkv = _qkv_proj(xt, mix_g, wqkv, vec(attn_q_norm), vec(attn_k_norm), i, j)
            a = _moba_attention(qkv, rel_bias, batch)
            xt = _wo_proj(a, xt, wo, j)
        xt = _ffn(xt, *ffn2, i)
    return xt.reshape(batch, seq, d)
```

```python
import functools
import math

import jax
import jax.numpy as jnp
from jax import lax
from jax.experimental import pallas as pl
from jax.experimental.pallas import tpu as pltpu

D_MODEL = 2048
N_HEADS = 16
HEAD_DIM = D_MODEL // N_HEADS
D_FF = 5632
CONV_WIDTH = 31
MOBA_BLOCK = 256
MOBA_TOPK = 3
NUM_BUCKETS = 32
MAX_DISTANCE = 2048
NORM_EPS = 1e-6
NEG_INF = -1e30
N_MIXERS = 2

LANES = 128
SUBLANES = 8
VMEM_LIMIT_BYTES = 60000 * 1024

BF16 = jnp.bfloat16
F32 = jnp.float32

FFN_TM = 1024
FFN_TF = 512
PROJ_TM = 1024
PROJ_TN = 512
QKV_TN = 1024
OUT_TM = 512
CONV_TS = 256
CONV_HALO = 32
CONV_ROWS = 128
CONV_PW2_COLS = 256
ATTN_HEADS_PER_STEP = 4
ATTN_CHUNK_BLOCKS = 2
ATTN_SUM_ROWS = 16
LOG2E = math.log2(math.e)


def _params(*semantics):
    return pltpu.CompilerParams(
        dimension_semantics=semantics, vmem_limit_bytes=VMEM_LIMIT_BYTES)


def _rms_norm_rows(x, g):
    ms = jnp.mean(x * x, axis=-1, keepdims=True)
    return (x * lax.rsqrt(ms + NORM_EPS)) * g


def _sigmoid(x):
    return 1.0 / (1.0 + jnp.exp(-x))


def _ffn_kernel(x_hbm, g_ref, wg_ref, wu_ref, wd_ref, o_ref, x_buf, h_ref, x_sem):
    i = pl.program_id(0)
    f = pl.program_id(1)
    tm = x_buf.shape[0]

    def x_copy(tile):
        rows = pl.ds(pl.multiple_of(tile * tm, tm), tm)
        return pltpu.make_async_copy(x_hbm.at[rows, :], x_buf, x_sem)

    @pl.when((i == 0) & (f == 0))
    def _():
        x_copy(0).start()

    @pl.when(f == 0)
    def _():
        x_copy(i).wait()
        x = x_buf[...]
        h_ref[...] = _rms_norm_rows(x, g_ref[...]).astype(BF16)
        o_ref[...] = x

    @pl.when((f == 1) & (i + 1 < pl.num_programs(0)))
    def _():
        x_copy(i + 1).start()

    h = h_ref[...]
    gate = jnp.dot(h, wg_ref[...], preferred_element_type=F32)
    up = jnp.dot(h, wu_ref[...], preferred_element_type=F32)
    act = (gate * _sigmoid(gate)) * up * 0.5
    o_ref[...] += jnp.dot(act.astype(BF16), wd_ref[...], preferred_element_type=F32)


def _ffn(x, norm_g, w_gate, w_up, w_down, layer):
    t, d = x.shape
    f = w_gate.shape[-1]
    tm, tf = FFN_TM, FFN_TF
    assert f // tf >= 2
    return pl.pallas_call(
        _ffn_kernel,
        grid=(t // tm, f // tf),
        in_specs=[
            pl.BlockSpec(memory_space=pl.ANY),
            pl.BlockSpec((None, 1, d), lambda i, j: (layer, 0, 0)),
            pl.BlockSpec((None, d, tf), lambda i, j: (layer, 0, j)),
            pl.BlockSpec((None, d, tf), lambda i, j: (layer, 0, j)),
            pl.BlockSpec((None, tf, d), lambda i, j: (layer, j, 0)),
        ],
        out_specs=pl.BlockSpec((tm, d), lambda i, j: (i, 0)),
        out_shape=jax.ShapeDtypeStruct((t, d), F32),
        scratch_shapes=[
            pltpu.VMEM((tm, d), F32),
            pltpu.VMEM((tm, d), BF16),
            pltpu.SemaphoreType.DMA(()),
        ],
        compiler_params=_params("arbitrary", "arbitrary"),
        name="ffn",
    )(x, norm_g, w_gate, w_up, w_down)


def _pw1_kernel(x_ref, g_ref, wa_ref, wg_ref, ba_ref, bg_ref, o_ref, h_ref):
    @pl.when(pl.program_id(1) == 0)
    def _():
        h_ref[...] = _rms_norm_rows(x_ref[...], g_ref[...]).astype(BF16)

    h = h_ref[...]
    a = jnp.dot(h, wa_ref[...].astype(BF16), preferred_element_type=F32) + ba_ref[...]
    g = jnp.dot(h, wg_ref[...].astype(BF16), preferred_element_type=F32) + bg_ref[...]
    o_ref[...] = a * _sigmoid(g)


def _pw1_glu(x, norm_g, w, b, layer, conv_layer):
    t, d = x.shape
    tm, tn = PROJ_TM, PROJ_TN
    nj = d // tn
    return pl.pallas_call(
        _pw1_kernel,
        grid=(t // tm, nj),
        in_specs=[
            pl.BlockSpec((tm, d), lambda i, j: (i, 0)),
            pl.BlockSpec((None, 1, d), lambda i, j: (layer, 0, 0)),
            pl.BlockSpec((None, d, tn), lambda i, j: (conv_layer, 0, j)),
            pl.BlockSpec((None, d, tn), lambda i, j: (conv_layer, 0, j + nj)),
            pl.BlockSpec((None, 1, tn), lambda i, j: (conv_layer, 0, j)),
            pl.BlockSpec((None, 1, tn), lambda i, j: (conv_layer, 0, j + nj)),
        ],
        out_specs=pl.BlockSpec((tm, tn), lambda i, j: (i, j)),
        out_shape=jax.ShapeDtypeStruct((t, d), F32),
        scratch_shapes=[pltpu.VMEM((tm, d), BF16)],
        compiler_params=_params("parallel", "arbitrary"),
        name="conv_pw1_glu",
    )(x, norm_g, w, w, b, b)


def _conv_kernel(tiles_per_seq, u_ref, halo_ref, dww_ref, dwb_ref, lng_ref, lnb_ref, w2_ref,
                 b2_ref, x_ref, o_ref, ext_ref, cv_ref, y_ref, w2b_ref):
    ts, d = u_ref.shape
    n_slabs = d // LANES
    pad = CONV_HALO - (CONV_WIDTH - 1)
    s = pl.program_id(0)
    conv_tile = jnp.minimum(s, pl.num_programs(0) - 2)
    first = lax.rem(conv_tile, tiles_per_seq) == 0

    @pl.when(s == 0)
    def _():
        y_ref[...] = jnp.zeros(y_ref.shape, BF16)
        w2b_ref[...] = w2_ref[...].astype(BF16)

    for c in range(n_slabs):
        lanes = slice(c * LANES, (c + 1) * LANES)
        halo = halo_ref[:, lanes]
        ext_ref[c, 0:CONV_HALO, :] = jnp.where(first, 0.0, halo)
        ext_ref[c, CONV_HALO:, :] = u_ref[:, lanes]

    slabs_per_piece = CONV_PW2_COLS // LANES
    for piece in range(d // CONV_PW2_COLS):
        for c in range(piece * slabs_per_piece, (piece + 1) * slabs_per_piece):
            row_starts = range(0, ts, CONV_ROWS)
            accs = [jnp.broadcast_to(dwb_ref[c], (CONV_ROWS, LANES)) for _ in row_starts]
            for k in range(CONV_WIDTH):
                w_row = dww_ref[c, k:k + 1, :]
                for a, r0 in enumerate(row_starts):
                    accs[a] = accs[a] + w_row * ext_ref[c, r0 + pad + k:r0 + pad + k + CONV_ROWS, :]
            for a, r0 in enumerate(row_starts):
                cv_ref[c, r0:r0 + CONV_ROWS, :] = accs[a]
        cols = slice(piece * CONV_PW2_COLS, (piece + 1) * CONV_PW2_COLS)
        o_ref[:, cols] = (x_ref[:, cols] + b2_ref[:, cols]
                          + jnp.dot(y_ref[...], w2b_ref[:, cols], preferred_element_type=F32))

    total = cv_ref[0]
    for c in range(1, n_slabs):
        total = total + cv_ref[c]
    mu = jnp.sum(total, axis=-1, keepdims=True) * (1.0 / d)
    sq = jnp.zeros((ts, LANES), F32)
    for c in range(n_slabs):
        xc = cv_ref[c] - mu
        sq = sq + xc * xc
    rstd = lax.rsqrt(jnp.sum(sq, axis=-1, keepdims=True) * (1.0 / d) + NORM_EPS)
    for c in range(n_slabs):
        lanes = slice(c * LANES, (c + 1) * LANES)
        y = (cv_ref[c] - mu) * rstd * lng_ref[:, lanes] + lnb_ref[:, lanes]
        y_ref[:, lanes] = (y * _sigmoid(y)).astype(BF16)


def _conv_module(u, x, dw_w, dw_b, ln_g, ln_b, w2, b2, conv_layer, batch):
    t, d = x.shape
    ts = CONV_TS
    n_slabs = d // LANES
    n_tiles = t // ts
    tiles_per_seq = t // batch // ts
    hpt = ts // CONV_HALO

    conv_row = lambda s: (jnp.minimum(s, n_tiles - 1), 0)
    halo_row = lambda s: (jnp.maximum(jnp.minimum(s, n_tiles - 1) * hpt - 1, 0), 0)
    pw2_row = lambda s: (jnp.maximum(s - 1, 0), 0)
    vec = lambda s: (conv_layer, 0, 0)
    return pl.pallas_call(
        functools.partial(_conv_kernel, tiles_per_seq),
        grid=(n_tiles + 1,),
        in_specs=[
            pl.BlockSpec((ts, d), conv_row),
            pl.BlockSpec((CONV_HALO, d), halo_row),
            pl.BlockSpec((None, n_slabs, CONV_WIDTH, LANES), lambda s: (conv_layer, 0, 0, 0)),
            pl.BlockSpec((None, n_slabs, 1, LANES), lambda s: (conv_layer, 0, 0, 0)),
            pl.BlockSpec((None, 1, d), vec),
            pl.BlockSpec((None, 1, d), vec),
            pl.BlockSpec((None, d, d), vec, pipeline_mode=pl.Buffered(1)),
            pl.BlockSpec((None, 1, d), vec),
            pl.BlockSpec((ts, d), pw2_row),
        ],
        out_specs=pl.BlockSpec((ts, d), pw2_row),
        out_shape=jax.ShapeDtypeStruct((t, d), F32),
        scratch_shapes=[
            pltpu.VMEM((n_slabs, ts + CONV_HALO, LANES), F32),
            pltpu.VMEM((n_slabs, ts, LANES), F32),
            pltpu.VMEM((ts, d), BF16),
            pltpu.VMEM((d, d), BF16),
        ],
        compiler_params=_params("arbitrary"),
        name="conv_dw_ln_pw2",
    )(u, u, dw_w, dw_b, ln_g, ln_b, w2, b2, x)


def _qkv_kernel(x_ref, g_ref, w_ref, qn_ref, kn_ref, o_ref, h_ref):
    j = pl.program_id(1)
    heads_per_step = o_ref.shape[0]
    n_q_steps = N_HEADS // heads_per_step

    @pl.when(j == 0)
    def _():
        h_ref[...] = _rms_norm_rows(x_ref[...], g_ref[...]).astype(BF16)

    y = jnp.dot(h_ref[...], w_ref[...].astype(BF16), preferred_element_type=F32)
    is_q = j < n_q_steps
    is_v = j >= 2 * n_q_steps
    gain = jnp.where(is_q, qn_ref[...] * (HEAD_DIM ** -0.5 * LOG2E), kn_ref[...])
    for c in range(heads_per_step):
        t = y[:, c * HEAD_DIM:(c + 1) * HEAD_DIM]
        normed = _rms_norm_rows(t, gain)
        o_ref[c] = jnp.where(is_v, t, normed).astype(BF16)


def _qkv_proj(x, norm_g, w, q_norm, k_norm, layer, attn_layer):
    t, d = x.shape
    tm, tn = PROJ_TM, QKV_TN
    hps = tn // HEAD_DIM
    return pl.pallas_call(
        _qkv_kernel,
        grid=(t // tm, 3 * d // tn),
        in_specs=[
            pl.BlockSpec((tm, d), lambda i, j: (i, 0)),
            pl.BlockSpec((None, 1, d), lambda i, j: (layer, 0, 0)),
            pl.BlockSpec((None, d, tn), lambda i, j: (attn_layer, 0, j)),
            pl.BlockSpec((None, 1, HEAD_DIM), lambda i, j: (attn_layer, 0, 0)),
            pl.BlockSpec((None, 1, HEAD_DIM), lambda i, j: (attn_layer, 0, 0)),
        ],
        out_specs=pl.BlockSpec((hps, tm, HEAD_DIM), lambda i, j: (j, i, 0)),
        out_shape=jax.ShapeDtypeStruct((3 * N_HEADS, t, HEAD_DIM), BF16),
        scratch_shapes=[pltpu.VMEM((tm, d), BF16)],
        compiler_params=_params("parallel", "arbitrary"),
        name="attn_qkv",
    )(x, norm_g, w, q_norm, k_norm)


def _t5_bucket(dist):
    max_exact = NUM_BUCKETS // 2
    n = jnp.maximum(dist, 0)
    nf = jnp.maximum(n, 1).astype(F32)
    large = max_exact + (jnp.log(nf / max_exact) / math.log(MAX_DISTANCE / max_exact)
                         * (NUM_BUCKETS - max_exact)).astype(jnp.int32)
    large = jnp.minimum(large, NUM_BUCKETS - 1)
    return jnp.where(n < max_exact, n, large)


def _bias_tile_cap(n_blocks):
    max_exact = NUM_BUCKETS // 2
    d = max_exact
    while max_exact + int(math.log(d / max_exact) / math.log(MAX_DISTANCE / max_exact)
                          * (NUM_BUCKETS - max_exact)) < NUM_BUCKETS - 1:
        d += 1
    cap = -(-(d + MOBA_BLOCK - 1) // MOBA_BLOCK)
    return min(cap, n_blocks - 1)


def _attn_kernel(tbl_ref, bucket_ref, q_ref, k_ref, v_ref, o_ref,
                 bias_ref, pen_ref, vt_ref, s0_ref, s1_ref, smax0_ref, smax1_ref,
                 m_ref, acc_ref):
    hg = pl.program_id(0)
    b = pl.program_id(1)
    i = pl.program_id(2)
    blk = MOBA_BLOCK
    n_heads, s_len, hd = q_ref.shape
    nb = s_len // blk
    chunk = ATTN_CHUNK_BLOCKS
    cw = chunk * blk
    n_tiles = bias_ref.shape[1]
    cap = n_tiles - (2 * chunk - 1)
    nt_dims = (((1,), (1,)), ((), ()))

    @pl.when((b == 0) & (i == 0))
    def _():
        x_idx = lax.broadcasted_iota(jnp.int32, (blk, blk), 0)
        y_idx = lax.broadcasted_iota(jnp.int32, (blk, blk), 1)
        for g in range(n_heads):
            h = hg * n_heads + g
            for j in range(n_tiles):
                dl = min(cap + chunk - 1 - j, cap)
                if dl < 0:
                    bias_ref[g, j] = jnp.full((blk, blk), NEG_INF, F32)
                    continue
                bk = bucket_ref[j:j + 1, :]
                w = jnp.zeros(bk.shape, F32)
                for bucket in range(NUM_BUCKETS):
                    w = jnp.where(bk == bucket, tbl_ref[h, bucket], w)
                rows = jnp.broadcast_to(w * LOG2E, (blk, 2 * blk))
                tile = pltpu.roll(rows, blk + 1, 1, stride=1, stride_axis=0)[:, :blk]
                if dl == 0:
                    tile = jnp.where(x_idx <= y_idx, tile, NEG_INF)
                bias_ref[g, j] = tile

    @pl.when(i == 0)
    def _():
        for g in range(n_heads):
            q = q_ref[g]
            k = k_ref[g]
            kmean = jnp.sum(k.astype(F32).reshape(nb, blk, hd), axis=1) * (1.0 / blk)
            km_hi = kmean.astype(BF16)
            km_lo = (kmean - km_hi.astype(F32)).astype(BF16)
            gate = (lax.dot_general(km_hi, q, nt_dims, preferred_element_type=F32)
                    + lax.dot_general(km_lo, q, nt_dims, preferred_element_type=F32))
            n_idx = lax.broadcasted_iota(jnp.int32, gate.shape, 0)
            own = lax.shift_right_logical(
                lax.broadcasted_iota(jnp.int32, gate.shape, 1), int(math.log2(blk)))
            cnt = jnp.zeros(gate.shape, jnp.int32)
            for n in range(nb):
                row = gate[n:n + 1, :]
                beats = jnp.where(row > gate, 1, jnp.where((row == gate) & (n < n_idx), 1, 0))
                cnt = cnt + jnp.where(n < own, beats, 0)
            keep = ((n_idx < own) & (cnt < MOBA_TOPK)) | (n_idx == own)
            pen = jnp.where(keep, 0.0, NEG_INF)
            for t in range(s_len // cw):
                pen_ref[g, t] = pen[:, t * cw:(t + 1) * cw]
            v = v_ref[g].astype(F32)
            ones_row = (lax.broadcasted_iota(jnp.int32, (ATTN_SUM_ROWS, cw), 0) == 0)
            for c in range(s_len // cw):
                vt_ref[g, c, :hd, :] = v[c * cw:(c + 1) * cw, :].T.astype(BF16)
                vt_ref[g, c, hd:, :] = jnp.where(ones_row, 1.0, 0.0).astype(BF16)

    q0 = pl.multiple_of(i * cw, cw)

    def pen_rows(g, c):
        return [pen_ref[g, i, pl.ds(c * chunk + kb, 1), :] for kb in range(chunk)]

    def logits(g, c, slot):
        s_ref, smax_ref = slots[slot]
        k0 = pl.multiple_of(c * cw, cw)
        s = lax.dot_general(k_ref[g, pl.ds(k0, cw), :], q_ref[g, pl.ds(q0, cw), :],
                            nt_dims, preferred_element_type=F32)
        pen = pen_rows(g, c)
        for qb in range(chunk):
            cols = slice(qb * blk, (qb + 1) * blk)
            j0 = jnp.maximum(cap + (chunk - 1) - ((i - c) * chunk + qb), 0)
            sq = s[:, cols] + bias_ref[g, pl.ds(j0, chunk)].reshape(cw, blk)
            s_ref[g, :, cols] = sq
            block_max = [jnp.max(sq[kb * blk:(kb + 1) * blk, :], axis=0, keepdims=True)
                         + pen[kb][:, cols] for kb in range(chunk)]
            smax_ref[g, :, cols] = functools.reduce(jnp.maximum, block_max)

    def softmax_step(g, c, slot):
        s_ref, smax_ref = slots[slot]
        m = m_ref[g]
        m_new = jnp.maximum(m, smax_ref[g])
        alpha = jnp.exp2(m - m_new)
        p = jnp.concatenate(
            [jnp.exp2(s_ref[g, kb * blk:(kb + 1) * blk, :] - jnp.where(pen_kb < 0.0, -NEG_INF, m_new))
             for kb, pen_kb in enumerate(pen_rows(g, c))], axis=0)
        m_ref[g] = m_new
        acc_ref[g] = alpha * acc_ref[g] + jnp.dot(vt_ref[g, c], p.astype(BF16),
                                                   preferred_element_type=F32)

    def step(c, slot):
        for g in range(n_heads):
            logits(g, c + 1, 1 - slot)
            softmax_step(g, c, slot)

    slots = ((s0_ref, smax0_ref), (s1_ref, smax1_ref))
    odd = i % 2
    for g in range(n_heads):
        m_ref[g] = jnp.full((1, cw), NEG_INF, F32)
        acc_ref[g] = jnp.zeros(acc_ref.shape[1:], F32)

    @pl.when(odd == 0)
    def _():
        for g in range(n_heads):
            logits(g, 0, 0)

    @pl.when(odd == 1)
    def _():
        for g in range(n_heads):
            logits(g, 0, 1)
        step(0, 1)

    def pair(j, carry):
        c = odd + 2 * j
        step(c, 0)
        step(c + 1, 1)
        return carry

    lax.fori_loop(0, i // 2, pair, 0)
    for g in range(n_heads):
        softmax_step(g, i, 0)
        acc = acc_ref[g]
        out = acc[:hd, :] * (1.0 / acc[hd:hd + 1, :])
        o_ref[:, g * hd:(g + 1) * hd] = out.T.astype(BF16)


def _moba_attention(qkv, rel_bias, batch):
    _, t, hd = qkv.shape
    s = t // batch
    blk = MOBA_BLOCK
    nb = s // blk
    g, chunk = ATTN_HEADS_PER_STEP, ATTN_CHUNK_BLOCKS
    groups = N_HEADS // g
    cap = _bias_tile_cap(nb)
    n_tiles = cap + 2 * chunk - 1
    dl = jnp.minimum(cap + (chunk - 1) - jnp.arange(n_tiles, dtype=jnp.int32), cap)[:, None]
    n = jnp.arange(2 * blk, dtype=jnp.int32)[None, :]
    buckets = _t5_bucket(dl * blk + n - (blk - 1))
    return pl.pallas_call(
        _attn_kernel,
        grid=(groups, batch, nb // chunk),
        in_specs=[
            pl.BlockSpec(memory_space=pltpu.SMEM),
            pl.BlockSpec((n_tiles, 2 * blk), lambda hg, b, i: (0, 0)),
            pl.BlockSpec((g, s, hd), lambda hg, b, i: (hg, b, 0)),
            pl.BlockSpec((g, s, hd), lambda hg, b, i: (groups + hg, b, 0)),
            pl.BlockSpec((g, s, hd), lambda hg, b, i: (2 * groups + hg, b, 0)),
        ],
        out_specs=pl.BlockSpec((chunk * blk, g * hd), lambda hg, b, i: (b * (nb // chunk) + i, hg)),
        out_shape=jax.ShapeDtypeStruct((t, N_HEADS * hd), BF16),
        scratch_shapes=[
            pltpu.VMEM((g, n_tiles, blk, blk), F32),
            pltpu.VMEM((g, s // (chunk * blk), nb, chunk * blk), F32),
            pltpu.VMEM((g, s // (chunk * blk), hd + ATTN_SUM_ROWS, chunk * blk), BF16),
            pltpu.VMEM((g, chunk * blk, chunk * blk), F32),
            pltpu.VMEM((g, chunk * blk, chunk * blk), F32),
            pltpu.VMEM((g, 1, chunk * blk), F32),
            pltpu.VMEM((g, 1, chunk * blk), F32),
            pltpu.VMEM((g, 1, chunk * blk), F32),
            pltpu.VMEM((g, hd + ATTN_SUM_ROWS, chunk * blk), F32),
        ],
        compiler_params=_params("arbitrary", "arbitrary", "arbitrary"),
        name="moba_attention",
    )(rel_bias.T.astype(F32), buckets, qkv, qkv, qkv)


def _wo_kernel(a_ref, w_ref, x_ref, o_ref, wb_ref):
    @pl.when(pl.program_id(0) == 0)
    def _():
        wb_ref[...] = w_ref[...].astype(BF16)

    o_ref[...] = x_ref[...] + jnp.dot(a_ref[...], wb_ref[...], preferred_element_type=F32)


def _wo_proj(a, x, w, attn_layer):
    t, d = x.shape
    tm = OUT_TM
    return pl.pallas_call(
        _wo_kernel,
        grid=(t // tm,),
        in_specs=[
            pl.BlockSpec((tm, d), lambda i: (i, 0)),
            pl.BlockSpec((None, d, d), lambda i: (attn_layer, 0, 0), pipeline_mode=pl.Buffered(1)),
            pl.BlockSpec((tm, d), lambda i: (i, 0)),
        ],
        out_specs=pl.BlockSpec((tm, d), lambda i: (i, 0)),
        out_shape=jax.ShapeDtypeStruct((t, d), F32),
        scratch_shapes=[pltpu.VMEM((d, d), BF16)],
        compiler_params=_params("arbitrary"),
        name="attn_wo",
    )(a, w, x)


def kernel(x, rel_bias, ffn1_norm, ffn1_w_gate, ffn1_w_up, ffn1_w_down, mix_norm, ffn2_norm, ffn2_w_gate, ffn2_w_up, ffn2_w_down, conv_pw1_w, conv_pw1_b, conv_dw_w, conv_dw_b, conv_ln_g, conv_ln_b, conv_pw2_w, conv_pw2_b, attn_wqkv, attn_q_norm, attn_k_norm, attn_wo):
    batch, seq, d = x.shape
    depth = ffn1_norm.shape[0]
    n_slabs = d // LANES
    vec = lambda v: v.reshape(v.shape[0], 1, v.shape[-1])
    bf = lambda w: w.astype(BF16)
    ffn1 = (vec(ffn1_norm), bf(ffn1_w_gate), bf(ffn1_w_up), bf(ffn1_w_down))
    ffn2 = (vec(ffn2_norm), bf(ffn2_w_gate), bf(ffn2_w_up), bf(ffn2_w_down))
    mix_g = vec(mix_norm)
    pw1_w, pw1_b = conv_pw1_w, vec(conv_pw1_b)
    nc = conv_dw_w.shape[0]
    dw_w = conv_dw_w.reshape(nc, CONV_WIDTH, n_slabs, LANES).transpose(0, 2, 1, 3)
    dw_b = conv_dw_b.reshape(nc, n_slabs, 1, LANES)
    pw2_w = conv_pw2_w
    wqkv, wo = attn_wqkv, attn_wo

    xt = x.reshape(batch * seq, d)
    for i in range(depth):
        xt = _ffn(xt, *ffn1, i)
        j = i // N_MIXERS
        if i % N_MIXERS == 0:
            u = _pw1_glu(xt, mix_g, pw1_w, pw1_b, i, j)
            xt = _conv_module(u, xt, dw_w, dw_b, vec(conv_ln_g), vec(conv_ln_b),
                              pw2_w, vec(conv_pw2_b), j, batch)
        else:
            qkv = _qkv_proj(xt, mix_g, wqkv, vec(attn_q_norm), vec(attn_k_norm), i, j)
            a = _moba_attention(qkv, rel_bias, batch)
            xt = _wo_proj(a, xt, wo, j)
        xt = _ffn(xt, *ffn2, i)
    return xt.reshape(batch, seq, d)
```

```python
import functools
import math

import jax
import jax.numpy as jnp
from jax import lax
from jax.experimental import pallas as pl
from jax.experimental.pallas import tpu as pltpu

D_MODEL = 2048
N_HEADS = 16
HEAD_DIM = D_MODEL // N_HEADS
D_FF = 5632
CONV_WIDTH = 31
MOBA_BLOCK = 256
MOBA_TOPK = 3
NUM_BUCKETS = 32
MAX_DISTANCE = 2048
NORM_EPS = 1e-6
NEG_INF = -1e30
N_MIXERS = 2

LANES = 128
SUBLANES = 8
VMEM_LIMIT_BYTES = 60000 * 1024

BF16 = jnp.bfloat16
F32 = jnp.float32

FFN_TM = 1024
FFN_TF = 512
PROJ_TM = 1024
PROJ_TN = 512
QKV_TN = 1024
OUT_TM = 512
CONV_TS = 256
CONV_HALO = 32
CONV_ROWS = 128
CONV_PW2_COLS = 256
ATTN_HEADS_PER_STEP = 4
ATTN_CHUNK_BLOCKS = 2
ATTN_SUM_ROWS = 16
LOG2E = math.log2(math.e)


def _params(*semantics):
    return pltpu.CompilerParams(
        dimension_semantics=semantics, vmem_limit_bytes=VMEM_LIMIT_BYTES)


def _rms_norm_rows(x, g):
    ms = jnp.mean(x * x, axis=-1, keepdims=True)
    return (x * lax.rsqrt(ms + NORM_EPS)) * g


def _sigmoid(x):
    return 1.0 / (1.0 + jnp.exp(-x))


def _ffn_kernel(x_hbm, g_ref, wg_ref, wu_ref, wd_ref, o_ref, x_buf, h_ref, x_sem):
    i = pl.program_id(0)
    f = pl.program_id(1)
    tm = x_buf.shape[0]

    def x_copy(tile):
        rows = pl.ds(pl.multiple_of(tile * tm, tm), tm)
        return pltpu.make_async_copy(x_hbm.at[rows, :], x_buf, x_sem)

    @pl.when((i == 0) & (f == 0))
    def _():
        x_copy(0).start()

    def down_proj(h):
        gate = jnp.dot(h, wg_ref[...].astype(BF16), preferred_element_type=F32)
        up = jnp.dot(h, wu_ref[...].astype(BF16), preferred_element_type=F32)
        act = (gate * _sigmoid(gate)) * up * 0.5
        return jnp.dot(act.astype(BF16), wd_ref[...].astype(BF16), preferred_element_type=F32)

    @pl.when(f == 0)
    def _():
        x_copy(i).wait()
        x = x_buf[...]
        h = _rms_norm_rows(x, g_ref[...]).astype(BF16)
        h_ref[...] = h
        o_ref[...] = x + down_proj(h)

    @pl.when(f > 0)
    def _():
        o_ref[...] += down_proj(h_ref[...])

    @pl.when((f == 1) & (i + 1 < pl.num_programs(0)))
    def _():
        x_copy(i + 1).start()


def _ffn(x, norm_g, w_gate, w_up, w_down, layer):
    t, d = x.shape
    f = w_gate.shape[-1]
    tm, tf = FFN_TM, FFN_TF
    assert f // tf >= 2
    return pl.pallas_call(
        _ffn_kernel,
        grid=(t // tm, f // tf),
        in_specs=[
            pl.BlockSpec(memory_space=pl.ANY),
            pl.BlockSpec((None, 1, d), lambda i, j: (layer, 0, 0)),
            pl.BlockSpec((None, d, tf), lambda i, j: (layer, 0, j)),
            pl.BlockSpec((None, d, tf), lambda i, j: (layer, 0, j)),
            pl.BlockSpec((None, tf, d), lambda i, j: (layer, j, 0)),
        ],
        out_specs=pl.BlockSpec((tm, d), lambda i, j: (i, 0)),
        out_shape=jax.ShapeDtypeStruct((t, d), F32),
        scratch_shapes=[
            pltpu.VMEM((tm, d), F32),
            pltpu.VMEM((tm, d), BF16),
            pltpu.SemaphoreType.DMA(()),
        ],
        compiler_params=_params("arbitrary", "arbitrary"),
        name="ffn",
    )(x, norm_g, w_gate, w_up, w_down)


def _pw1_kernel(x_ref, g_ref, wa_ref, wg_ref, ba_ref, bg_ref, o_ref, h_ref):
    @pl.when(pl.program_id(1) == 0)
    def _():
        h_ref[...] = _rms_norm_rows(x_ref[...], g_ref[...]).astype(BF16)

    h = h_ref[...]
    a = jnp.dot(h, wa_ref[...].astype(BF16), preferred_element_type=F32) + ba_ref[...]
    g = jnp.dot(h, wg_ref[...].astype(BF16), preferred_element_type=F32) + bg_ref[...]
    o_ref[...] = a * _sigmoid(g)


def _pw1_glu(x, norm_g, w, b, layer, conv_layer):
    t, d = x.shape
    tm, tn = PROJ_TM, PROJ_TN
    nj = d // tn
    return pl.pallas_call(
        _pw1_kernel,
        grid=(t // tm, nj),
        in_specs=[
            pl.BlockSpec((tm, d), lambda i, j: (i, 0)),
            pl.BlockSpec((None, 1, d), lambda i, j: (layer, 0, 0)),
            pl.BlockSpec((None, d, tn), lambda i, j: (conv_layer, 0, j)),
            pl.BlockSpec((None, d, tn), lambda i, j: (conv_layer, 0, j + nj)),
            pl.BlockSpec((None, 1, tn), lambda i, j: (conv_layer, 0, j)),
            pl.BlockSpec((None, 1, tn), lambda i, j: (conv_layer, 0, j + nj)),
        ],
        out_specs=pl.BlockSpec((tm, tn), lambda i, j: (i, j)),
        out_shape=jax.ShapeDtypeStruct((t, d), F32),
        scratch_shapes=[pltpu.VMEM((tm, d), BF16)],
        compiler_params=_params("parallel", "arbitrary"),
        name="conv_pw1_glu",
    )(x, norm_g, w, w, b, b)


def _conv_kernel(tiles_per_seq, u_ref, halo_ref, dww_ref, dwb_ref, lng_ref, lnb_ref, w2_ref,
                 b2_ref, x_ref, o_ref, ext_ref, cv_ref, y_ref, w2b_ref):
    ts, d = u_ref.shape
    n_slabs = d // LANES
    pad = CONV_HALO - (CONV_WIDTH - 1)
    s = pl.program_id(0)
    conv_tile = jnp.minimum(s, pl.num_programs(0) - 2)
    first = lax.rem(conv_tile, tiles_per_seq) == 0

    @pl.when(s == 0)
    def _():
        y_ref[...] = jnp.zeros(y_ref.shape, BF16)
        w2b_ref[...] = w2_ref[...].astype(BF16)

    for c in range(n_slabs):
        lanes = slice(c * LANES, (c + 1) * LANES)
        halo = halo_ref[:, lanes]
        ext_ref[c, 0:CONV_HALO, :] = jnp.where(first, 0.0, halo)
        ext_ref[c, CONV_HALO:, :] = u_ref[:, lanes]

    slabs_per_piece = CONV_PW2_COLS // LANES
    for piece in range(d // CONV_PW2_COLS):
        for c in range(piece * slabs_per_piece, (piece + 1) * slabs_per_piece):
            row_starts = range(0, ts, CONV_ROWS)
            accs = [jnp.broadcast_to(dwb_ref[c], (CONV_ROWS, LANES)) for _ in row_starts]
            for k in range(CONV_WIDTH):
                w_row = dww_ref[c, k:k + 1, :]
                for a, r0 in enumerate(row_starts):
                    accs[a] = accs[a] + w_row * ext_ref[c, r0 + pad + k:r0 + pad + k + CONV_ROWS, :]
            for a, r0 in enumerate(row_starts):
                cv_ref[c, r0:r0 + CONV_ROWS, :] = accs[a]
        cols = slice(piece * CONV_PW2_COLS, (piece + 1) * CONV_PW2_COLS)
        o_ref[:, cols] = (x_ref[:, cols] + b2_ref[:, cols]
                          + jnp.dot(y_ref[...], w2b_ref[:, cols], preferred_element_type=F32))

    total = cv_ref[0]
    for c in range(1, n_slabs):
        total = total + cv_ref[c]
    mu = jnp.sum(total, axis=-1, keepdims=True) * (1.0 / d)
    sq = jnp.zeros((ts, LANES), F32)
    for c in range(n_slabs):
        xc = cv_ref[c] - mu
        sq = sq + xc * xc
    rstd = lax.rsqrt(jnp.sum(sq, axis=-1, keepdims=True) * (1.0 / d) + NORM_EPS)
    for c in range(n_slabs):
        lanes = slice(c * LANES, (c + 1) * LANES)
        y = (cv_ref[c] - mu) * rstd * lng_ref[:, lanes] + lnb_ref[:, lanes]
        y_ref[:, lanes] = (y * _sigmoid(y)).astype(BF16)


def _conv_module(u, x, dw_w, dw_b, ln_g, ln_b, w2, b2, conv_layer, batch):
    t, d = x.shape
    ts = CONV_TS
    n_slabs = d // LANES
    n_tiles = t // ts
    tiles_per_seq = t // batch // ts
    hpt = ts // CONV_HALO

    conv_row = lambda s: (jnp.minimum(s, n_tiles - 1), 0)
    halo_row = lambda s: (jnp.maximum(jnp.minimum(s, n_tiles - 1) * hpt - 1, 0), 0)
    pw2_row = lambda s: (jnp.maximum(s - 1, 0), 0)
    vec = lambda s: (conv_layer, 0, 0)
    return pl.pallas_call(
        functools.partial(_conv_kernel, tiles_per_seq),
        grid=(n_tiles + 1,),
        in_specs=[
            pl.BlockSpec((ts, d), conv_row),
            pl.BlockSpec((CONV_HALO, d), halo_row),
            pl.BlockSpec((None, n_slabs, CONV_WIDTH, LANES), lambda s: (conv_layer, 0, 0, 0)),
            pl.BlockSpec((None, n_slabs, 1, LANES), lambda s: (conv_layer, 0, 0, 0)),
            pl.BlockSpec((None, 1, d), vec),
            pl.BlockSpec((None, 1, d), vec),
            pl.BlockSpec((None, d, d), vec, pipeline_mode=pl.Buffered(1)),
            pl.BlockSpec((None, 1, d), vec),
            pl.BlockSpec((ts, d), pw2_row),
        ],
        out_specs=pl.BlockSpec((ts, d), pw2_row),
        out_shape=jax.ShapeDtypeStruct((t, d), F32),
        scratch_shapes=[
            pltpu.VMEM((n_slabs, ts + CONV_HALO, LANES), F32),
            pltpu.VMEM((n_slabs, ts, LANES), F32),
            pltpu.VMEM((ts, d), BF16),
            pltpu.VMEM((d, d), BF16),
        ],
        compiler_params=_params("arbitrary"),
        name="conv_dw_ln_pw2",
    )(u, u, dw_w, dw_b, ln_g, ln_b, w2, b2, x)


def _qkv_kernel(x_ref, g_ref, w_ref, qn_ref, kn_ref, o_ref, h_ref):
    j = pl.program_id(1)
    heads_per_step = o_ref.shape[0]
    n_q_steps = N_HEADS // heads_per_step

    @pl.when(j == 0)
    def _():
        h_ref[...] = _rms_norm_rows(x_ref[...], g_ref[...]).astype(BF16)

    y = jnp.dot(h_ref[...], w_ref[...].astype(BF16), preferred_element_type=F32)
    is_q = j < n_q_steps
    is_v = j >= 2 * n_q_steps
    gain = jnp.where(is_q, qn_ref[...] * (HEAD_DIM ** -0.5 * LOG2E), kn_ref[...])
    for c in range(heads_per_step):
        t = y[:, c * HEAD_DIM:(c + 1) * HEAD_DIM]
        normed = _rms_norm_rows(t, gain)
        o_ref[c] = jnp.where(is_v, t, normed).astype(BF16)


def _qkv_proj(x, norm_g, w, q_norm, k_norm, layer, attn_layer):
    t, d = x.shape
    tm, tn = PROJ_TM, QKV_TN
    hps = tn // HEAD_DIM
    return pl.pallas_call(
        _qkv_kernel,
        grid=(t // tm, 3 * d // tn),
        in_specs=[
            pl.BlockSpec((tm, d), lambda i, j: (i, 0)),
            pl.BlockSpec((None, 1, d), lambda i, j: (layer, 0, 0)),
            pl.BlockSpec((None, d, tn), lambda i, j: (attn_layer, 0, j)),
            pl.BlockSpec((None, 1, HEAD_DIM), lambda i, j: (attn_layer, 0, 0)),
            pl.BlockSpec((None, 1, HEAD_DIM), lambda i, j: (attn_layer, 0, 0)),
        ],
        out_specs=pl.BlockSpec((hps, tm, HEAD_DIM), lambda i, j: (j, i, 0)),
        out_shape=jax.ShapeDtypeStruct((3 * N_HEADS, t, HEAD_DIM), BF16),
        scratch_shapes=[pltpu.VMEM((tm, d), BF16)],
        compiler_params=_params("parallel", "arbitrary"),
        name="attn_qkv",
    )(x, norm_g, w, q_norm, k_norm)


def _t5_bucket(dist):
    max_exact = NUM_BUCKETS // 2
    n = jnp.maximum(dist, 0)
    nf = jnp.maximum(n, 1).astype(F32)
    large = max_exact + (jnp.log(nf / max_exact) / math.log(MAX_DISTANCE / max_exact)
                         * (NUM_BUCKETS - max_exact)).astype(jnp.int32)
    large = jnp.minimum(large, NUM_BUCKETS - 1)
    return jnp.where(n < max_exact, n, large)


def _bias_tile_cap(n_blocks):
    max_exact = NUM_BUCKETS // 2
    d = max_exact
    while max_exact + int(math.log(d / max_exact) / math.log(MAX_DISTANCE / max_exact)
                          * (NUM_BUCKETS - max_exact)) < NUM_BUCKETS - 1:
        d += 1
    cap = -(-(d + MOBA_BLOCK - 1) // MOBA_BLOCK)
    return min(cap, n_blocks - 1)


def _attn_kernel(tbl_ref, bucket_ref, q_ref, k_ref, v_ref, o_ref,
                 bias_ref, pen_ref, vt_ref, s0_ref, s1_ref, smax0_ref, smax1_ref,
                 m_ref, acc_ref):
    hg = pl.program_id(0)
    b = pl.program_id(1)
    i = pl.program_id(2)
    blk = MOBA_BLOCK
    n_heads, s_len, hd = q_ref.shape
    nb = s_len // blk
    chunk = ATTN_CHUNK_BLOCKS
    cw = chunk * blk
    n_tiles = bias_ref.shape[1]
    cap = n_tiles - (2 * chunk - 1)
    nt_dims = (((1,), (1,)), ((), ()))

    @pl.when((b == 0) & (i == 0))
    def _():
        x_idx = lax.broadcasted_iota(jnp.int32, (blk, blk), 0)
        y_idx = lax.broadcasted_iota(jnp.int32, (blk, blk), 1)
        for g in range(n_heads):
            h = hg * n_heads + g
            for j in range(n_tiles):
                dl = min(cap + chunk - 1 - j, cap)
                if dl < 0:
                    bias_ref[g, j] = jnp.full((blk, blk), NEG_INF, F32)
                    continue
                bk = bucket_ref[j:j + 1, :]
                w = jnp.zeros(bk.shape, F32)
                for bucket in range(NUM_BUCKETS):
                    w = jnp.where(bk == bucket, tbl_ref[h, bucket], w)
                rows = jnp.broadcast_to(w * LOG2E, (blk, 2 * blk))
                tile = pltpu.roll(rows, blk + 1, 1, stride=1, stride_axis=0)[:, :blk]
                if dl == 0:
                    tile = jnp.where(x_idx <= y_idx, tile, NEG_INF)
                bias_ref[g, j] = tile

    @pl.when(i == 0)
    def _():
        for g in range(n_heads):
            q = q_ref[g]
            k = k_ref[g]
            kmean = jnp.sum(k.astype(F32).reshape(nb, blk, hd), axis=1) * (1.0 / blk)
            km_hi = kmean.astype(BF16)
            km_lo = (kmean - km_hi.astype(F32)).astype(BF16)
            gate = (lax.dot_general(km_hi, q, nt_dims, preferred_element_type=F32)
                    + lax.dot_general(km_lo, q, nt_dims, preferred_element_type=F32))
            n_idx = lax.broadcasted_iota(jnp.int32, gate.shape, 0)
            own = lax.shift_right_logical(
                lax.broadcasted_iota(jnp.int32, gate.shape, 1), int(math.log2(blk)))
            cnt = jnp.zeros(gate.shape, jnp.int32)
            for n in range(nb):
                row = gate[n:n + 1, :]
                beats = jnp.where(row > gate, 1, jnp.where((row == gate) & (n < n_idx), 1, 0))
                cnt = cnt + jnp.where(n < own, beats, 0)
            keep = ((n_idx < own) & (cnt < MOBA_TOPK)) | (n_idx == own)
            pen = jnp.where(keep, 0.0, NEG_INF)
            for t in range(s_len // cw):
                pen_ref[g, t] = pen[:, t * cw:(t + 1) * cw]
            v = v_ref[g].astype(F32)
            ones_row = (lax.broadcasted_iota(jnp.int32, (ATTN_SUM_ROWS, cw), 0) == 0)
            for c in range(s_len // cw):
                vt_ref[g, c, :hd, :] = v[c * cw:(c + 1) * cw, :].T.astype(BF16)
                vt_ref[g, c, hd:, :] = jnp.where(ones_row, 1.0, 0.0).astype(BF16)

    q0 = pl.multiple_of(i * cw, cw)

    def pen_rows(g, c):
        return [pen_ref[g, i, pl.ds(c * chunk + kb, 1), :] for kb in range(chunk)]

    def logits(g, c, slot):
        s_ref, smax_ref = slots[slot]
        k0 = pl.multiple_of(c * cw, cw)
        s = lax.dot_general(k_ref[g, pl.ds(k0, cw), :], q_ref[g, pl.ds(q0, cw), :],
                            nt_dims, preferred_element_type=F32)
        pen = pen_rows(g, c)
        for qb in range(chunk):
            cols = slice(qb * blk, (qb + 1) * blk)
            j0 = jnp.maximum(cap + (chunk - 1) - ((i - c) * chunk + qb), 0)
            sq = s[:, cols] + bias_ref[g, pl.ds(j0, chunk)].reshape(cw, blk)
            s_ref[g, :, cols] = sq
            block_max = [jnp.max(sq[kb * blk:(kb + 1) * blk, :], axis=0, keepdims=True)
                         + pen[kb][:, cols] for kb in range(chunk)]
            smax_ref[g, :, cols] = functools.reduce(jnp.maximum, block_max)

    def softmax_step(g, c, slot):
        s_ref, smax_ref = slots[slot]
        m = m_ref[g]
        m_new = jnp.maximum(m, smax_ref[g])
        alpha = jnp.exp2(m - m_new)
        p = jnp.concatenate(
            [jnp.exp2(s_ref[g, kb * blk:(kb + 1) * blk, :] - jnp.where(pen_kb < 0.0, -NEG_INF, m_new))
             for kb, pen_kb in enumerate(pen_rows(g, c))], axis=0)
        m_ref[g] = m_new
        acc_ref[g] = alpha * acc_ref[g] + jnp.dot(vt_ref[g, c], p.astype(BF16),
                                                   preferred_element_type=F32)

    def step(c, slot):
        for g in range(n_heads):
            logits(g, c + 1, 1 - slot)
            softmax_step(g, c, slot)

    slots = ((s0_ref, smax0_ref), (s1_ref, smax1_ref))
    odd = i % 2
    for g in range(n_heads):
        m_ref[g] = jnp.full((1, cw), NEG_INF, F32)
        acc_ref[g] = jnp.zeros(acc_ref.shape[1:], F32)

    @pl.when(odd == 0)
    def _():
        for g in range(n_heads):
            logits(g, 0, 0)

    @pl.when(odd == 1)
    def _():
        for g in range(n_heads):
            logits(g, 0, 1)
        step(0, 1)

    def pair(j, carry):
        c = odd + 2 * j
        step(c, 0)
        step(c + 1, 1)
        return carry

    lax.fori_loop(0, i // 2, pair, 0)
    for g in range(n_heads):
        softmax_step(g, i, 0)
        acc = acc_ref[g]
        out = acc[:hd, :] * (1.0 / acc[hd:hd + 1, :])
        o_ref[:, g * hd:(g + 1) * hd] = out.T.astype(BF16)


def _moba_attention(qkv, rel_bias, batch):
    _, t, hd = qkv.shape
    s = t // batch
    blk = MOBA_BLOCK
    nb = s // blk
    g, chunk = ATTN_HEADS_PER_STEP, ATTN_CHUNK_BLOCKS
    groups = N_HEADS // g
    cap = _bias_tile_cap(nb)
    n_tiles = cap + 2 * chunk - 1
    dl = jnp.minimum(cap + (chunk - 1) - jnp.arange(n_tiles, dtype=jnp.int32), cap)[:, None]
    n = jnp.arange(2 * blk, dtype=jnp.int32)[None, :]
    buckets = _t5_bucket(dl * blk + n - (blk - 1))
    return pl.pallas_call(
        _attn_kernel,
        grid=(groups, batch, nb // chunk),
        in_specs=[
            pl.BlockSpec(memory_space=pltpu.SMEM),
            pl.BlockSpec((n_tiles, 2 * blk), lambda hg, b, i: (0, 0)),
            pl.BlockSpec((g, s, hd), lambda hg, b, i: (hg, b, 0)),
            pl.BlockSpec((g, s, hd), lambda hg, b, i: (groups + hg, b, 0)),
            pl.BlockSpec((g, s, hd), lambda hg, b, i: (2 * groups + hg, b, 0)),
        ],
        out_specs=pl.BlockSpec((chunk * blk, g * hd), lambda hg, b, i: (b * (nb // chunk) + i, hg)),
        out_shape=jax.ShapeDtypeStruct((t, N_HEADS * hd), BF16),
        scratch_shapes=[
            pltpu.VMEM((g, n_tiles, blk, blk), F32),
            pltpu.VMEM((g, s // (chunk * blk), nb, chunk * blk), F32),
            pltpu.VMEM((g, s // (chunk * blk), hd + ATTN_SUM_ROWS, chunk * blk), BF16),
            pltpu.VMEM((g, chunk * blk, chunk * blk), F32),
            pltpu.VMEM((g, chunk * blk, chunk * blk), F32),
            pltpu.VMEM((g, 1, chunk * blk), F32),
            pltpu.VMEM((g, 1, chunk * blk), F32),
            pltpu.VMEM((g, 1, chunk * blk), F32),
            pltpu.VMEM((g, hd + ATTN_SUM_ROWS, chunk * blk), F32),
        ],
        compiler_params=_params("arbitrary", "arbitrary", "arbitrary"),
        name="moba_attention",
    )(rel_bias.T.astype(F32), buckets, qkv, qkv, qkv)


def _wo_kernel(a_ref, w_ref, x_ref, o_ref, wb_ref):
    @pl.when(pl.program_id(0) == 0)
    def _():
        wb_ref[...] = w_ref[...].astype(BF16)

    o_ref[...] = x_ref[...] + jnp.dot(a_ref[...], wb_ref[...], preferred_element_type=F32)


def _wo_proj(a, x, w, attn_layer):
    t, d = x.shape
    tm = OUT_TM
    return pl.pallas_call(
        _wo_kernel,
        grid=(t // tm,),
        in_specs=[
            pl.BlockSpec((tm, d), lambda i: (i, 0)),
            pl.BlockSpec((None, d, d), lambda i: (attn_layer, 0, 0), pipeline_mode=pl.Buffered(1)),
            pl.BlockSpec((tm, d), lambda i: (i, 0)),
        ],
        out_specs=pl.BlockSpec((tm, d), lambda i: (i, 0)),
        out_shape=jax.ShapeDtypeStruct((t, d), F32),
        scratch_shapes=[pltpu.VMEM((d, d), BF16)],
        compiler_params=_params("arbitrary"),
        name="attn_wo",
    )(a, w, x)


def kernel(x, rel_bias, ffn1_norm, ffn1_w_gate, ffn1_w_up, ffn1_w_down, mix_norm, ffn2_norm, ffn2_w_gate, ffn2_w_up, ffn2_w_down, conv_pw1_w, conv_pw1_b, conv_dw_w, conv_dw_b, conv_ln_g, conv_ln_b, conv_pw2_w, conv_pw2_b, attn_wqkv, attn_q_norm, attn_k_norm, attn_wo):
    batch, seq, d = x.shape
    depth = ffn1_norm.shape[0]
    n_slabs = d // LANES
    vec = lambda v: v.reshape(v.shape[0], 1, v.shape[-1])
    ffn1 = (vec(ffn1_norm), ffn1_w_gate, ffn1_w_up, ffn1_w_down)
    ffn2 = (vec(ffn2_norm), ffn2_w_gate, ffn2_w_up, ffn2_w_down)
    mix_g = vec(mix_norm)
    pw1_w, pw1_b = conv_pw1_w, vec(conv_pw1_b)
    nc = conv_dw_w.shape[0]
    dw_w = conv_dw_w.reshape(nc, CONV_WIDTH, n_slabs, LANES).transpose(0, 2, 1, 3)
    dw_b = conv_dw_b.reshape(nc, n_slabs, 1, LANES)
    pw2_w = conv_pw2_w
    wqkv, wo = attn_wqkv, attn_wo

    xt = x.reshape(batch * seq, d)
    for i in range(depth):
        xt = _ffn(xt, *ffn1, i)
        j = i // N_MIXERS
        if i % N_MIXERS == 0:
            u = _pw1_glu(xt, mix_g, pw1_w, pw1_b, i, j)
            xt = _conv_module(u, xt, dw_w, dw_b, vec(conv_ln_g), vec(conv_ln_b),
                              pw2_w, vec(conv_pw2_b), j, batch)
        else:
            qkv = _qkv_proj(xt, mix_g, wqkv, vec(attn_q_norm), vec(attn_k_norm), i, j)
            a = _moba_attention(qkv, rel_bias, batch)
            xt = _wo_proj(a, xt, wo, j)
        xt = _ffn(xt, *ffn2, i)
    return xt.reshape(batch, seq, d)
```

```python
import functools
import math

import jax
import jax.numpy as jnp
from jax import lax
from jax.experimental import pallas as pl
from jax.experimental.pallas import tpu as pltpu

D_MODEL = 2048
N_HEADS = 16
HEAD_DIM = D_MODEL // N_HEADS
D_FF = 5632
CONV_WIDTH = 31
MOBA_BLOCK = 256
MOBA_TOPK = 3
NUM_BUCKETS = 32
MAX_DISTANCE = 2048
NORM_EPS = 1e-6
NEG_INF = -1e30
N_MIXERS = 2

LANES = 128
SUBLANES = 8
VMEM_LIMIT_BYTES = 60000 * 1024

BF16 = jnp.bfloat16
F32 = jnp.float32

FFN_TM = 1024
FFN_TF = 512
PROJ_TM = 1024
PROJ_TN = 512
QKV_TN = 1024
OUT_TM = 512
CONV_TS = 256
CONV_HALO = 32
CONV_ROWS = 128
CONV_PW2_COLS = 256
ATTN_HEADS_PER_STEP = 4
ATTN_CHUNK_BLOCKS = 2
ATTN_SUM_ROWS = 16
LOG2E = math.log2(math.e)


def _params(*semantics):
    return pltpu.CompilerParams(
        dimension_semantics=semantics, vmem_limit_bytes=VMEM_LIMIT_BYTES)


def _rms_norm_rows(x, g):
    ms = jnp.mean(x * x, axis=-1, keepdims=True)
    return (x * lax.rsqrt(ms + NORM_EPS)) * g


def _sigmoid(x):
    return 1.0 / (1.0 + jnp.exp(-x))


def _ffn_kernel(x_hbm, g_ref, wg_ref, wu_ref, wd_ref, o_ref, x_buf, h_ref, x_sem):
    i = pl.program_id(0)
    f = pl.program_id(1)
    tm = x_buf.shape[0]

    def x_copy(tile):
        rows = pl.ds(pl.multiple_of(tile * tm, tm), tm)
        return pltpu.make_async_copy(x_hbm.at[rows, :], x_buf, x_sem)

    @pl.when((i == 0) & (f == 0))
    def _():
        x_copy(0).start()

    def down_proj(h):
        gate = jnp.dot(h, wg_ref[...].astype(BF16), preferred_element_type=F32)
        up = jnp.dot(h, wu_ref[...].astype(BF16), preferred_element_type=F32)
        act = (gate * _sigmoid(gate)) * up * 0.5
        return jnp.dot(act.astype(BF16), wd_ref[...].astype(BF16), preferred_element_type=F32)

    @pl.when(f == 0)
    def _():
        x_copy(i).wait()
        x = x_buf[...]
        h = _rms_norm_rows(x, g_ref[...]).astype(BF16)
        h_ref[...] = h
        o_ref[...] = x + down_proj(h)

    @pl.when(f > 0)
    def _():
        o_ref[...] += down_proj(h_ref[...])

    @pl.when((f == 1) & (i + 1 < pl.num_programs(0)))
    def _():
        x_copy(i + 1).start()


def _ffn(x, norm_g, w_gate, w_up, w_down, layer):
    t, d = x.shape
    f = w_gate.shape[-1]
    tm, tf = FFN_TM, FFN_TF
    assert f // tf >= 2
    return pl.pallas_call(
        _ffn_kernel,
        grid=(t // tm, f // tf),
        in_specs=[
            pl.BlockSpec(memory_space=pl.ANY),
            pl.BlockSpec((None, 1, d), lambda i, j: (layer, 0, 0)),
            pl.BlockSpec((None, d, tf), lambda i, j: (layer, 0, j)),
            pl.BlockSpec((None, d, tf), lambda i, j: (layer, 0, j)),
            pl.BlockSpec((None, tf, d), lambda i, j: (layer, j, 0)),
        ],
        out_specs=pl.BlockSpec((tm, d), lambda i, j: (i, 0)),
        out_shape=jax.ShapeDtypeStruct((t, d), F32),
        scratch_shapes=[
            pltpu.VMEM((tm, d), F32),
            pltpu.VMEM((tm, d), BF16),
            pltpu.SemaphoreType.DMA(()),
        ],
        compiler_params=_params("arbitrary", "arbitrary"),
        name="ffn",
    )(x, norm_g, w_gate, w_up, w_down)


def _pw1_kernel(x_ref, g_ref, wa_ref, wg_ref, ba_ref, bg_ref, o_ref, h_ref):
    def glu(h):
        a = jnp.dot(h, wa_ref[...].astype(BF16), preferred_element_type=F32) + ba_ref[...]
        g = jnp.dot(h, wg_ref[...].astype(BF16), preferred_element_type=F32) + bg_ref[...]
        o_ref[...] = a * _sigmoid(g)

    @pl.when(pl.program_id(1) == 0)
    def _():
        h = _rms_norm_rows(x_ref[...], g_ref[...]).astype(BF16)
        h_ref[...] = h
        glu(h)

    @pl.when(pl.program_id(1) > 0)
    def _():
        glu(h_ref[...])


def _pw1_glu(x, norm_g, w, b, layer, conv_layer):
    t, d = x.shape
    tm, tn = PROJ_TM, PROJ_TN
    nj = d // tn
    return pl.pallas_call(
        _pw1_kernel,
        grid=(t // tm, nj),
        in_specs=[
            pl.BlockSpec((tm, d), lambda i, j: (i, 0)),
            pl.BlockSpec((None, 1, d), lambda i, j: (layer, 0, 0)),
            pl.BlockSpec((None, d, tn), lambda i, j: (conv_layer, 0, j)),
            pl.BlockSpec((None, d, tn), lambda i, j: (conv_layer, 0, j + nj)),
            pl.BlockSpec((None, 1, tn), lambda i, j: (conv_layer, 0, j)),
            pl.BlockSpec((None, 1, tn), lambda i, j: (conv_layer, 0, j + nj)),
        ],
        out_specs=pl.BlockSpec((tm, tn), lambda i, j: (i, j)),
        out_shape=jax.ShapeDtypeStruct((t, d), F32),
        scratch_shapes=[pltpu.VMEM((tm, d), BF16)],
        compiler_params=_params("parallel", "arbitrary"),
        name="conv_pw1_glu",
    )(x, norm_g, w, w, b, b)


def _conv_kernel(tiles_per_seq, u_ref, halo_ref, dww_ref, dwb_ref, lng_ref, lnb_ref, w2_ref,
                 b2_ref, x_ref, o_ref, ext_ref, cv_ref, y_ref, w2b_ref):
    ts, d = u_ref.shape
    n_slabs = d // LANES
    pad = CONV_HALO - (CONV_WIDTH - 1)
    s = pl.program_id(0)
    conv_tile = jnp.minimum(s, pl.num_programs(0) - 2)
    first = lax.rem(conv_tile, tiles_per_seq) == 0

    @pl.when(s == 0)
    def _():
        y_ref[...] = jnp.zeros(y_ref.shape, BF16)
        w2b_ref[...] = w2_ref[...].astype(BF16)

    for c in range(n_slabs):
        lanes = slice(c * LANES, (c + 1) * LANES)
        halo = halo_ref[:, lanes]
        ext_ref[c, 0:CONV_HALO, :] = jnp.where(first, 0.0, halo)
        ext_ref[c, CONV_HALO:, :] = u_ref[:, lanes]

    slabs_per_piece = CONV_PW2_COLS // LANES
    for piece in range(d // CONV_PW2_COLS):
        for c in range(piece * slabs_per_piece, (piece + 1) * slabs_per_piece):
            row_starts = range(0, ts, CONV_ROWS)
            accs = [jnp.broadcast_to(dwb_ref[c], (CONV_ROWS, LANES)) for _ in row_starts]
            for k in range(CONV_WIDTH):
                w_row = dww_ref[c, k:k + 1, :]
                for a, r0 in enumerate(row_starts):
                    accs[a] = accs[a] + w_row * ext_ref[c, r0 + pad + k:r0 + pad + k + CONV_ROWS, :]
            for a, r0 in enumerate(row_starts):
                cv_ref[c, r0:r0 + CONV_ROWS, :] = accs[a]
        cols = slice(piece * CONV_PW2_COLS, (piece + 1) * CONV_PW2_COLS)
        o_ref[:, cols] = (x_ref[:, cols] + b2_ref[:, cols]
                          + jnp.dot(y_ref[...], w2b_ref[:, cols], preferred_element_type=F32))

    total = cv_ref[0]
    for c in range(1, n_slabs):
        total = total + cv_ref[c]
    mu = jnp.sum(total, axis=-1, keepdims=True) * (1.0 / d)
    sq = jnp.zeros((ts, LANES), F32)
    for c in range(n_slabs):
        xc = cv_ref[c] - mu
        sq = sq + xc * xc
    rstd = lax.rsqrt(jnp.sum(sq, axis=-1, keepdims=True) * (1.0 / d) + NORM_EPS)
    for c in range(n_slabs):
        lanes = slice(c * LANES, (c + 1) * LANES)
        y = (cv_ref[c] - mu) * rstd * lng_ref[:, lanes] + lnb_ref[:, lanes]
        y_ref[:, lanes] = (y * _sigmoid(y)).astype(BF16)


def _conv_module(u, x, dw_w, dw_b, ln_g, ln_b, w2, b2, conv_layer, batch):
    t, d = x.shape
    ts = CONV_TS
    n_slabs = d // LANES
    n_tiles = t // ts
    tiles_per_seq = t // batch // ts
    hpt = ts // CONV_HALO

    conv_row = lambda s: (jnp.minimum(s, n_tiles - 1), 0)
    halo_row = lambda s: (jnp.maximum(jnp.minimum(s, n_tiles - 1) * hpt - 1, 0), 0)
    pw2_row = lambda s: (jnp.maximum(s - 1, 0), 0)
    vec = lambda s: (conv_layer, 0, 0)
    return pl.pallas_call(
        functools.partial(_conv_kernel, tiles_per_seq),
        grid=(n_tiles + 1,),
        in_specs=[
            pl.BlockSpec((ts, d), conv_row),
            pl.BlockSpec((CONV_HALO, d), halo_row),
            pl.BlockSpec((None, n_slabs, CONV_WIDTH, LANES), lambda s: (conv_layer, 0, 0, 0)),
            pl.BlockSpec((None, n_slabs, 1, LANES), lambda s: (conv_layer, 0, 0, 0)),
            pl.BlockSpec((None, 1, d), vec),
            pl.BlockSpec((None, 1, d), vec),
            pl.BlockSpec((None, d, d), vec, pipeline_mode=pl.Buffered(1)),
            pl.BlockSpec((None, 1, d), vec),
            pl.BlockSpec((ts, d), pw2_row),
        ],
        out_specs=pl.BlockSpec((ts, d), pw2_row),
        out_shape=jax.ShapeDtypeStruct((t, d), F32),
        scratch_shapes=[
            pltpu.VMEM((n_slabs, ts + CONV_HALO, LANES), F32),
            pltpu.VMEM((n_slabs, ts, LANES), F32),
            pltpu.VMEM((ts, d), BF16),
            pltpu.VMEM((d, d), BF16),
        ],
        compiler_params=_params("arbitrary"),
        name="conv_dw_ln_pw2",
    )(u, u, dw_w, dw_b, ln_g, ln_b, w2, b2, x)


def _qkv_kernel(x_ref, g_ref, w_ref, qn_ref, kn_ref, o_ref, h_ref):
    j = pl.program_id(1)
    heads_per_step = o_ref.shape[0]
    n_q_steps = N_HEADS // heads_per_step

    def project(h):
        y = jnp.dot(h, w_ref[...].astype(BF16), preferred_element_type=F32)
        is_q = j < n_q_steps
        is_v = j >= 2 * n_q_steps
        gain = jnp.where(is_q, qn_ref[...] * (HEAD_DIM ** -0.5 * LOG2E), kn_ref[...])
        for c in range(heads_per_step):
            t = y[:, c * HEAD_DIM:(c + 1) * HEAD_DIM]
            normed = _rms_norm_rows(t, gain)
            o_ref[c] = jnp.where(is_v, t, normed).astype(BF16)

    @pl.when(j == 0)
    def _():
        h = _rms_norm_rows(x_ref[...], g_ref[...]).astype(BF16)
        h_ref[...] = h
        project(h)

    @pl.when(j > 0)
    def _():
        project(h_ref[...])


def _qkv_proj(x, norm_g, w, q_norm, k_norm, layer, attn_layer):
    t, d = x.shape
    tm, tn = PROJ_TM, QKV_TN
    hps = tn // HEAD_DIM
    return pl.pallas_call(
        _qkv_kernel,
        grid=(t // tm, 3 * d // tn),
        in_specs=[
            pl.BlockSpec((tm, d), lambda i, j: (i, 0)),
            pl.BlockSpec((None, 1, d), lambda i, j: (layer, 0, 0)),
            pl.BlockSpec((None, d, tn), lambda i, j: (attn_layer, 0, j)),
            pl.BlockSpec((None, 1, HEAD_DIM), lambda i, j: (attn_layer, 0, 0)),
            pl.BlockSpec((None, 1, HEAD_DIM), lambda i, j: (attn_layer, 0, 0)),
        ],
        out_specs=pl.BlockSpec((hps, tm, HEAD_DIM), lambda i, j: (j, i, 0)),
        out_shape=jax.ShapeDtypeStruct((3 * N_HEADS, t, HEAD_DIM), BF16),
        scratch_shapes=[pltpu.VMEM((tm, d), BF16)],
        compiler_params=_params("parallel", "arbitrary"),
        name="attn_qkv",
    )(x, norm_g, w, q_norm, k_norm)


def _t5_bucket(dist):
    max_exact = NUM_BUCKETS // 2
    n = jnp.maximum(dist, 0)
    nf = jnp.maximum(n, 1).astype(F32)
    large = max_exact + (jnp.log(nf / max_exact) / math.log(MAX_DISTANCE / max_exact)
                         * (NUM_BUCKETS - max_exact)).astype(jnp.int32)
    large = jnp.minimum(large, NUM_BUCKETS - 1)
    return jnp.where(n < max_exact, n, large)


def _bias_tile_cap(n_blocks):
    max_exact = NUM_BUCKETS // 2
    d = max_exact
    while max_exact + int(math.log(d / max_exact) / math.log(MAX_DISTANCE / max_exact)
                          * (NUM_BUCKETS - max_exact)) < NUM_BUCKETS - 1:
        d += 1
    cap = -(-(d + MOBA_BLOCK - 1) // MOBA_BLOCK)
    return min(cap, n_blocks - 1)


def _attn_kernel(tbl_ref, bucket_ref, q_ref, k_ref, v_ref, o_ref,
                 bias_ref, pen_ref, vt_ref, s0_ref, s1_ref, smax0_ref, smax1_ref,
                 m_ref, acc_ref):
    hg = pl.program_id(0)
    b = pl.program_id(1)
    i = pl.program_id(2)
    blk = MOBA_BLOCK
    n_heads, s_len, hd = q_ref.shape
    nb = s_len // blk
    chunk = ATTN_CHUNK_BLOCKS
    cw = chunk * blk
    n_tiles = bias_ref.shape[1]
    cap = n_tiles - (2 * chunk - 1)
    nt_dims = (((1,), (1,)), ((), ()))

    @pl.when((b == 0) & (i == 0))
    def _():
        x_idx = lax.broadcasted_iota(jnp.int32, (blk, blk), 0)
        y_idx = lax.broadcasted_iota(jnp.int32, (blk, blk), 1)
        for g in range(n_heads):
            h = hg * n_heads + g
            for j in range(n_tiles):
                dl = min(cap + chunk - 1 - j, cap)
                if dl < 0:
                    bias_ref[g, j] = jnp.full((blk, blk), NEG_INF, F32)
                    continue
                bk = bucket_ref[j:j + 1, :]
                w = jnp.zeros(bk.shape, F32)
                for bucket in range(NUM_BUCKETS):
                    w = jnp.where(bk == bucket, tbl_ref[h, bucket], w)
                rows = jnp.broadcast_to(w * LOG2E, (blk, 2 * blk))
                tile = pltpu.roll(rows, blk + 1, 1, stride=1, stride_axis=0)[:, :blk]
                if dl == 0:
                    tile = jnp.where(x_idx <= y_idx, tile, NEG_INF)
                bias_ref[g, j] = tile

    @pl.when(i == 0)
    def _():
        for g in range(n_heads):
            q = q_ref[g]
            k = k_ref[g]
            kmean = jnp.sum(k.astype(F32).reshape(nb, blk, hd), axis=1) * (1.0 / blk)
            km_hi = kmean.astype(BF16)
            km_lo = (kmean - km_hi.astype(F32)).astype(BF16)
            gate = (lax.dot_general(km_hi, q, nt_dims, preferred_element_type=F32)
                    + lax.dot_general(km_lo, q, nt_dims, preferred_element_type=F32))
            n_idx = lax.broadcasted_iota(jnp.int32, gate.shape, 0)
            own = lax.shift_right_logical(
                lax.broadcasted_iota(jnp.int32, gate.shape, 1), int(math.log2(blk)))
            cnt = jnp.zeros(gate.shape, jnp.int32)
            for n in range(nb):
                row = gate[n:n + 1, :]
                beats = jnp.where(row > gate, 1, jnp.where((row == gate) & (n < n_idx), 1, 0))
                cnt = cnt + jnp.where(n < own, beats, 0)
            keep = ((n_idx < own) & (cnt < MOBA_TOPK)) | (n_idx == own)
            pen = jnp.where(keep, 0.0, NEG_INF)
            for t in range(s_len // cw):
                pen_ref[g, t] = pen[:, t * cw:(t + 1) * cw]
            v = v_ref[g].astype(F32)
            ones_row = (lax.broadcasted_iota(jnp.int32, (ATTN_SUM_ROWS, cw), 0) == 0)
            for c in range(s_len // cw):
                vt_ref[g, c, :hd, :] = v[c * cw:(c + 1) * cw, :].T.astype(BF16)
                vt_ref[g, c, hd:, :] = jnp.where(ones_row, 1.0, 0.0).astype(BF16)

    q0 = pl.multiple_of(i * cw, cw)

    def pen_rows(g, c):
        return [pen_ref[g, i, pl.ds(c * chunk + kb, 1), :] for kb in range(chunk)]

    def logits(g, c, slot):
        s_ref, smax_ref = slots[slot]
        k0 = pl.multiple_of(c * cw, cw)
        s = lax.dot_general(k_ref[g, pl.ds(k0, cw), :], q_ref[g, pl.ds(q0, cw), :],
                            nt_dims, preferred_element_type=F32)
        pen = pen_rows(g, c)
        for qb in range(chunk):
            cols = slice(qb * blk, (qb + 1) * blk)
            j0 = jnp.maximum(cap + (chunk - 1) - ((i - c) * chunk + qb), 0)
            sq = s[:, cols] + bias_ref[g, pl.ds(j0, chunk)].reshape(cw, blk)
            s_ref[g, :, cols] = sq
            block_max = [jnp.max(sq[kb * blk:(kb + 1) * blk, :], axis=0, keepdims=True)
                         + pen[kb][:, cols] for kb in range(chunk)]
            smax_ref[g, :, cols] = functools.reduce(jnp.maximum, block_max)

    def softmax_step(g, c, slot):
        s_ref, smax_ref = slots[slot]
        m = m_ref[g]
        m_new = jnp.maximum(m, smax_ref[g])
        alpha = jnp.exp2(m - m_new)
        p = jnp.concatenate(
            [jnp.exp2(s_ref[g, kb * blk:(kb + 1) * blk, :] - jnp.where(pen_kb < 0.0, -NEG_INF, m_new))
             for kb, pen_kb in enumerate(pen_rows(g, c))], axis=0)
        m_ref[g] = m_new
        acc_ref[g] = alpha * acc_ref[g] + jnp.dot(vt_ref[g, c], p.astype(BF16),
                                                   preferred_element_type=F32)

    def step(c, slot):
        for g in range(n_heads):
            logits(g, c + 1, 1 - slot)
            softmax_step(g, c, slot)

    slots = ((s0_ref, smax0_ref), (s1_ref, smax1_ref))
    odd = i % 2
    for g in range(n_heads):
        m_ref[g] = jnp.full((1, cw), NEG_INF, F32)
        acc_ref[g] = jnp.zeros(acc_ref.shape[1:], F32)

    @pl.when(odd == 0)
    def _():
        for g in range(n_heads):
            logits(g, 0, 0)

    @pl.when(odd == 1)
    def _():
        for g in range(n_heads):
            logits(g, 0, 1)
        step(0, 1)

    def pair(j, carry):
        c = odd + 2 * j
        step(c, 0)
        step(c + 1, 1)
        return carry

    lax.fori_loop(0, i // 2, pair, 0)
    for g in range(n_heads):
        softmax_step(g, i, 0)
        acc = acc_ref[g]
        out = acc[:hd, :] * (1.0 / acc[hd:hd + 1, :])
        o_ref[:, g * hd:(g + 1) * hd] = out.T.astype(BF16)


def _moba_attention(qkv, rel_bias, batch):
    _, t, hd = qkv.shape
    s = t // batch
    blk = MOBA_BLOCK
    nb = s // blk
    g, chunk = ATTN_HEADS_PER_STEP, ATTN_CHUNK_BLOCKS
    groups = N_HEADS // g
    cap = _bias_tile_cap(nb)
    n_tiles = cap + 2 * chunk - 1
    dl = jnp.minimum(cap + (chunk - 1) - jnp.arange(n_tiles, dtype=jnp.int32), cap)[:, None]
    n = jnp.arange(2 * blk, dtype=jnp.int32)[None, :]
    buckets = _t5_bucket(dl * blk + n - (blk - 1))
    return pl.pallas_call(
        _attn_kernel,
        grid=(groups, batch, nb // chunk),
        in_specs=[
            pl.BlockSpec(memory_space=pltpu.SMEM),
            pl.BlockSpec((n_tiles, 2 * blk), lambda hg, b, i: (0, 0)),
            pl.BlockSpec((g, s, hd), lambda hg, b, i: (hg, b, 0)),
            pl.BlockSpec((g, s, hd), lambda hg, b, i: (groups + hg, b, 0)),
            pl.BlockSpec((g, s, hd), lambda hg, b, i: (2 * groups + hg, b, 0)),
        ],
        out_specs=pl.BlockSpec((chunk * blk, g * hd), lambda hg, b, i: (b * (nb // chunk) + i, hg)),
        out_shape=jax.ShapeDtypeStruct((t, N_HEADS * hd), BF16),
        scratch_shapes=[
            pltpu.VMEM((g, n_tiles, blk, blk), F32),
            pltpu.VMEM((g, s // (chunk * blk), nb, chunk * blk), F32),
            pltpu.VMEM((g, s // (chunk * blk), hd + ATTN_SUM_ROWS, chunk * blk), BF16),
            pltpu.VMEM((g, chunk * blk, chunk * blk), F32),
            pltpu.VMEM((g, chunk * blk, chunk * blk), F32),
            pltpu.VMEM((g, 1, chunk * blk), F32),
            pltpu.VMEM((g, 1, chunk * blk), F32),
            pltpu.VMEM((g, 1, chunk * blk), F32),
            pltpu.VMEM((g, hd + ATTN_SUM_ROWS, chunk * blk), F32),
        ],
        compiler_params=_params("arbitrary", "arbitrary", "arbitrary"),
        name="moba_attention",
    )(rel_bias.T.astype(F32), buckets, qkv, qkv, qkv)


def _wo_kernel(a_ref, w_ref, x_ref, o_ref, wb_ref):
    @pl.when(pl.program_id(0) == 0)
    def _():
        wb_ref[...] = w_ref[...].astype(BF16)

    o_ref[...] = x_ref[...] + jnp.dot(a_ref[...], wb_ref[...], preferred_element_type=F32)


def _wo_proj(a, x, w, attn_layer):
    t, d = x.shape
    tm = OUT_TM
    return pl.pallas_call(
        _wo_kernel,
        grid=(t // tm,),
        in_specs=[
            pl.BlockSpec((tm, d), lambda i: (i, 0)),
            pl.BlockSpec((None, d, d), lambda i: (attn_layer, 0, 0), pipeline_mode=pl.Buffered(1)),
            pl.BlockSpec((tm, d), lambda i: (i, 0)),
        ],
        out_specs=pl.BlockSpec((tm, d), lambda i: (i, 0)),
        out_shape=jax.ShapeDtypeStruct((t, d), F32),
        scratch_shapes=[pltpu.VMEM((d, d), BF16)],
        compiler_params=_params("arbitrary"),
        name="attn_wo",
    )(a, w, x)


def kernel(x, rel_bias, ffn1_norm, ffn1_w_gate, ffn1_w_up, ffn1_w_down, mix_norm, ffn2_norm, ffn2_w_gate, ffn2_w_up, ffn2_w_down, conv_pw1_w, conv_pw1_b, conv_dw_w, conv_dw_b, conv_ln_g, conv_ln_b, conv_pw2_w, conv_pw2_b, attn_wqkv, attn_q_norm, attn_k_norm, attn_wo):
    batch, seq, d = x.shape
    depth = ffn1_norm.shape[0]
    n_slabs = d // LANES
    vec = lambda v: v.reshape(v.shape[0], 1, v.shape[-1])
    ffn1 = (vec(ffn1_norm), ffn1_w_gate, ffn1_w_up, ffn1_w_down)
    ffn2 = (vec(ffn2_norm), ffn2_w_gate, ffn2_w_up, ffn2_w_down)
    mix_g = vec(mix_norm)
    pw1_w, pw1_b = conv_pw1_w, vec(conv_pw1_b)
    nc = conv_dw_w.shape[0]
    dw_w = conv_dw_w.reshape(nc, CONV_WIDTH, n_slabs, LANES).transpose(0, 2, 1, 3)
    dw_b = conv_dw_b.reshape(nc, n_slabs, 1, LANES)
    pw2_w = conv_pw2_w
    wqkv, wo = attn_wqkv, attn_wo

    xt = x.reshape(batch * seq, d)
    for i in range(depth):
        xt = _ffn(xt, *ffn1, i)
        j = i // N_MIXERS
        if i % N_MIXERS == 0:
            u = _pw1_glu(xt, mix_g, pw1_w, pw1_b, i, j)
            xt = _conv_module(u, xt, dw_w, dw_b, vec(conv_ln_g), vec(conv_ln_b),
                              pw2_w, vec(conv_pw2_b), j, batch)
        else:
            qkv = _qkv_proj(xt, mix_g, wqkv, vec(attn_q_norm), vec(attn_k_norm), i, j)
            a = _moba_attention(qkv, rel_bias, batch)
            xt = _wo_proj(a, xt, wo, j)
        xt = _ffn(xt, *ffn2, i)
    return xt.reshape(batch, seq, d)
```
